```python
import math
import jax
import jax.numpy as jnp
from jax import lax
import numpy as np


D_MODEL = 4096
BATCH = 2
SEQ = 8192
DEPTH = 2

HEAD_DIM = 128
SBA_HEADS = D_MODEL // 256
GDN_HEADS = D_MODEL // 256
GDN_DK = 128
GDN_DV = 128
GDN_CHUNK = 64
CONV_K = 4
DIFF_HEADS = D_MODEL // 256
DIFF_DH = 64
Q_BLOCK = 128
ROPE_THETA = 10000.0
N_GROUPS = 4
EXPERTS_PER_GROUP = 8
N_EXPERTS = N_GROUPS * EXPERTS_PER_GROUP
TOP_K = 2
D_EXPERT = D_MODEL // 8
LN_EPS = 1e-5
NORM_EPS = 1e-6
DEEPNORM_ALPHA = (2 * DEPTH) ** 0.25
DEEPNORM_BETA = (8 * DEPTH) ** -0.25

SBA_W = SBA_HEADS * HEAD_DIM
GDN_QK_W = GDN_HEADS * GDN_DK
GDN_W = GDN_HEADS * GDN_DV
GDN_CONV_CH = 2 * GDN_QK_W + GDN_W
DIFF_W = DIFF_HEADS * 2 * DIFF_DH
IN_SPLIT_SIZES = (3 * SBA_W, GDN_CONV_CH, GDN_W, GDN_HEADS, GDN_HEADS, 3 * DIFF_W, 3 * D_MODEL)
IN_SPLIT_POINTS = tuple(sum(IN_SPLIT_SIZES[:i + 1]) for i in range(len(IN_SPLIT_SIZES) - 1))
N_IN = sum(IN_SPLIT_SIZES)

kernel_name = 'hybrid_sba_gdn_diff_hmoe_deepnorm'


def layer_norm(x, g, b):
    xf = x.astype(jnp.float32)
    mu = jnp.mean(xf, axis=-1, keepdims=True)
    var = jnp.mean(jnp.square(xf - mu), axis=-1, keepdims=True)
    y = (xf - mu) * lax.rsqrt(var + LN_EPS) * g.astype(jnp.float32) + b.astype(jnp.float32)
    return y.astype(x.dtype)


def rms_norm(x, w, eps):
    xf = x.astype(jnp.float32)
    return xf * lax.rsqrt(jnp.mean(xf * xf, axis=-1, keepdims=True) + eps) * w.astype(jnp.float32)


def l2_normalize(x):
    return x * lax.rsqrt(jnp.sum(x * x, axis=-1, keepdims=True) + NORM_EPS)


def rope_tables(seq, dim):
    pos = jnp.arange(seq, dtype=jnp.float32)
    inv_freq = ROPE_THETA ** (-jnp.arange(0, dim, 2, dtype=jnp.float32) / dim)
    ang = pos[:, None] * inv_freq[None, :]
    ang = jnp.concatenate([ang, ang], axis=-1)
    return jnp.cos(ang), jnp.sin(ang)


def apply_rope(x, cos, sin):
    xf = x.astype(jnp.float32)
    half = xf.shape[-1] // 2
    rot = jnp.concatenate([-xf[..., half:], xf[..., :half]], axis=-1)
    c = cos[None, None, :, None, :]
    s = sin[None, None, :, None, :]
    return (xf * c + rot * s).astype(x.dtype)


def to_query_blocks(t):
    b, h, s = t.shape[:3]
    t = t.reshape((b, h, s // Q_BLOCK, Q_BLOCK) + t.shape[3:])
    return jnp.moveaxis(t, 2, 0)


def from_query_blocks(o):
    nb, b, h, q, dv = o.shape
    return o.transpose(1, 0, 3, 2, 4).reshape(b, nb * q, h * dv)


def stick_breaking_attention(q, k, v):
    seq = q.shape[2]
    scale = HEAD_DIM ** -0.5
    key_pos = jnp.arange(seq)

    def block(args):
        q_blk, blk = args
        z = jnp.einsum('bhqd,bhkd->bhqk', q_blk, k).astype(jnp.float32) * scale
        q_pos = blk * Q_BLOCK + jnp.arange(Q_BLOCK)
        strict = key_pos[None, :] < q_pos[:, None]
        log_1m_beta = jnp.where(strict, jax.nn.log_sigmoid(-z), 0.0)
        later = lax.cumsum(log_1m_beta, axis=3, reverse=True) - log_1m_beta
        att = jnp.where(strict, jnp.exp(jax.nn.log_sigmoid(z) + later), 0.0)
        return jnp.einsum('bhqk,bhkd->bhqd', att.astype(v.dtype), v)

    out = lax.map(block, (to_query_blocks(q), jnp.arange(seq // Q_BLOCK)))
    return from_query_blocks(out)


def differential_attention(q, k, v, lam, subln_w, lam_init):
    seq = q.shape[2]
    scale = DIFF_DH ** -0.5
    key_pos = jnp.arange(seq)

    def block(args):
        q_blk, blk = args
        s = jnp.einsum('bhqcd,bhkcd->bhcqk', q_blk, k).astype(jnp.float32) * scale
        q_pos = blk * Q_BLOCK + jnp.arange(Q_BLOCK)
        causal = key_pos[None, :] <= q_pos[:, None]
        p = jax.nn.softmax(jnp.where(causal, s, -jnp.inf), axis=-1)
        att = p[:, :, 0] - lam * p[:, :, 1]
        return jnp.einsum('bhqk,bhkd->bhqd', att.astype(v.dtype), v)

    out = lax.map(block, (to_query_blocks(q), jnp.arange(seq // Q_BLOCK)))
    out = rms_norm(out, subln_w, LN_EPS) * (1.0 - lam_init)
    return from_query_blocks(out.astype(v.dtype))


def causal_depthwise_conv(x, w):
    seq = x.shape[1]
    xp = jnp.pad(x, ((0, 0), (CONV_K - 1, 0), (0, 0)))
    y = xp[:, 0:seq, :] * w[0]
    for j in range(1, CONV_K):
        y = y + xp[:, j:j + seq, :] * w[j]
    return y


def chunk_gated_delta_rule(q, k, v, g, beta):
    bsz, seq, h, dk = q.shape
    dv = v.shape[-1]
    c = GDN_CHUNK
    n = seq // c

    def chunks(t):
        t = t.reshape((bsz, n, c, h) + t.shape[3:])
        return jnp.moveaxis(t, 3, 1)

    q, k, v = chunks(q), chunks(k), chunks(v)
    g = jnp.cumsum(chunks(g), axis=-1)
    beta = chunks(beta)
    idx = jnp.arange(c)
    incl = idx[:, None] >= idx[None, :]
    strict = idx[:, None] > idx[None, :]
    gdiff = g[..., :, None] - g[..., None, :]
    decay = jnp.where(incl, jnp.exp(jnp.where(incl, gdiff, 0.0)), 0.0)
    k_beta = k * beta[..., None]
    lower = jnp.where(strict, jnp.einsum('bhncd,bhnkd->bhnck', k_beta, k) * decay, 0.0)
    system = lower + jnp.eye(c, dtype=jnp.float32)
    u = lax.linalg.triangular_solve(system, v * beta[..., None], left_side=True, lower=True, unit_diagonal=True)
    w = lax.linalg.triangular_solve(system, k_beta * jnp.exp(g)[..., None], left_side=True, lower=True, unit_diagonal=True)
    intra = jnp.einsum('bhncd,bhnkd->bhnck', q, k) * decay
    q_dec = q * jnp.exp(g)[..., None]
    k_dec = k * jnp.exp(g[..., -1:] - g)[..., None]
    chunk_decay = jnp.exp(g[..., -1])

    def step(state, xs):
        u_c, w_c, q_c, k_c, a_c, d_c = xs
        v_new = u_c - jnp.einsum('bhck,bhkv->bhcv', w_c, state)
        o_c = jnp.einsum('bhck,bhkv->bhcv', q_c, state) + jnp.einsum('bhcs,bhsv->bhcv', a_c, v_new)
        state = state * d_c[..., None, None] + jnp.einsum('bhck,bhcv->bhkv', k_c, v_new)
        return state, o_c

    xs = tuple(jnp.moveaxis(t, 2, 0) for t in (u, w, q_dec, k_dec, intra, chunk_decay))
    state0 = jnp.zeros((bsz, h, dk, dv), jnp.float32)
    _, o = lax.scan(step, state0, xs)
    return o.transpose(1, 0, 3, 2, 4).reshape(bsz, seq, h, dv)


def gated_deltanet(qkv, z, a, b, conv_w, a_log, dt_bias, norm_w):
    bsz, seq, _ = qkv.shape
    out_dtype = qkv.dtype
    qkv = jax.nn.silu(causal_depthwise_conv(qkv, conv_w)).astype(jnp.float32)
    q, k, v = jnp.split(qkv, [GDN_QK_W, 2 * GDN_QK_W], axis=-1)
    q = l2_normalize(q.reshape(bsz, seq, GDN_HEADS, GDN_DK)) * (GDN_DK ** -0.5)
    k = l2_normalize(k.reshape(bsz, seq, GDN_HEADS, GDN_DK))
    v = v.reshape(bsz, seq, GDN_HEADS, GDN_DV)
    beta = jax.nn.sigmoid(b.astype(jnp.float32))
    g = -jnp.exp(a_log.astype(jnp.float32)) * jax.nn.softplus(a.astype(jnp.float32) + dt_bias.astype(jnp.float32))
    o = chunk_gated_delta_rule(q, k, v, g, beta)
    zf = z.astype(jnp.float32).reshape(bsz, seq, GDN_HEADS, GDN_DV)
    o = rms_norm(o, norm_w, NORM_EPS) * jax.nn.silu(zf)
    return o.reshape(bsz, seq, GDN_W).astype(out_dtype)


def hybrid_mixer(x, w_in, conv_w, gdn_a_log, gdn_dt_bias, gdn_norm_w, lq1, lk1, lq2, lk2, subln_w,
                 w_branch_sba, w_branch_gdn, w_branch_diff, w_out, cos, sin, layer):
    bsz, seq, _ = x.shape
    proj = jnp.einsum('bsd,dn->bsn', x, w_in)
    sba_qkv, gdn_qkv, gdn_z, gdn_a, gdn_b, diff_qkv, gate_logits = jnp.split(proj, IN_SPLIT_POINTS, axis=-1)

    def heads(t, h, d):
        return t.reshape(bsz, seq, h, d).transpose(0, 2, 1, 3)
    sq, sk, sv = jnp.split(sba_qkv, 3, axis=-1)
    y_sba = stick_breaking_attention(heads(sq, SBA_HEADS, HEAD_DIM), heads(sk, SBA_HEADS, HEAD_DIM),
                                     heads(sv, SBA_HEADS, HEAD_DIM))

    y_gdn = gated_deltanet(gdn_qkv, gdn_z, gdn_a, gdn_b, conv_w, gdn_a_log, gdn_dt_bias, gdn_norm_w)

    dq, dk, dv = jnp.split(diff_qkv, 3, axis=-1)
    dq = dq.reshape(bsz, seq, DIFF_HEADS, 2, DIFF_DH).transpose(0, 2, 1, 3, 4)
    dk = dk.reshape(bsz, seq, DIFF_HEADS, 2, DIFF_DH).transpose(0, 2, 1, 3, 4)
    dv = heads(dv, DIFF_HEADS, 2 * DIFF_DH)
    lam_init = 0.8 - 0.6 * math.exp(-0.3 * layer)
    lam = (jnp.exp(jnp.sum(lq1.astype(jnp.float32) * lk1.astype(jnp.float32)))
           - jnp.exp(jnp.sum(lq2.astype(jnp.float32) * lk2.astype(jnp.float32))) + lam_init)
    y_diff = differential_attention(apply_rope(dq, cos, sin), apply_rope(dk, cos, sin), dv, lam, subln_w, lam_init)

    g_sba, g_gdn, g_diff = jnp.split(jax.nn.sigmoid(gate_logits), 3, axis=-1)
    merged = (g_sba * jnp.einsum('bsw,wd->bsd', y_sba, w_branch_sba)
              + g_gdn * jnp.einsum('bsw,wd->bsd', y_gdn, w_branch_gdn)
              + g_diff * jnp.einsum('bsw,wd->bsd', y_diff, w_branch_diff))
    return jnp.einsum('bsd,de->bse', merged, w_out)


def hierarchical_moe(x, w_rg, b_rg, w_re, b_re, w_gate, w_up, w_down):
    bsz, seq, d = x.shape
    t = bsz * seq
    xt = x.reshape(t, d)
    p_group = jax.nn.softmax(jnp.einsum('td,dg->tg', xt, w_rg).astype(jnp.float32) + b_rg.astype(jnp.float32), axis=-1)
    g_w, g_idx = lax.top_k(p_group, 1)
    e_logits = (jnp.einsum('td,de->te', xt, w_re).astype(jnp.float32) + b_re.astype(jnp.float32))
    e_logits = e_logits.reshape(t, N_GROUPS, EXPERTS_PER_GROUP)
    sel = e_logits[jnp.arange(t), g_idx[:, 0]]
    e_w, e_idx = lax.top_k(jax.nn.softmax(sel, axis=-1), TOP_K)
    weights = g_w * (e_w / jnp.sum(e_w, axis=-1, keepdims=True))
    expert_id = g_idx * EXPERTS_PER_GROUP + e_idx
    combine = jnp.sum(jax.nn.one_hot(expert_id, N_EXPERTS, dtype=jnp.float32) * weights[..., None], axis=1)
    combine = combine.reshape(t, N_GROUPS, EXPERTS_PER_GROUP)
    out = jnp.zeros((t, d), jnp.float32)
    for gi in range(N_GROUPS):
        sl = slice(gi * EXPERTS_PER_GROUP, (gi + 1) * EXPERTS_PER_GROUP)
        h = jax.nn.silu(jnp.einsum('td,edf->tef', xt, w_gate[sl])) * jnp.einsum('td,edf->tef', xt, w_up[sl])
        h = h * combine[:, gi, :, None].astype(h.dtype)
        out = out + jnp.einsum('tef,efd->td', h, w_down[sl]).astype(jnp.float32)
    return out.reshape(bsz, seq, d).astype(x.dtype)


def setup_inputs(seed: int = 0) -> dict:
    key = jax.random.key(seed)
    ks = iter(jax.random.split(key, 32))
    f32 = jnp.float32

    def nrm(shape, scale):
        return jax.random.normal(next(ks), shape, f32) * scale

    def gain(shape):
        return 1.0 + nrm(shape, 0.02)

    x = nrm((BATCH, SEQ, D_MODEL), 1.0)
    w_in = nrm((DEPTH, D_MODEL, N_IN), D_MODEL ** -0.5)
    conv_w = nrm((DEPTH, CONV_K, GDN_CONV_CH), CONV_K ** -0.5)
    gdn_a_log = jnp.log(jax.random.uniform(next(ks), (DEPTH, GDN_HEADS), f32, 1.0, 16.0))
    dt = jnp.exp(jax.random.uniform(next(ks), (DEPTH, GDN_HEADS), f32, math.log(1e-3), math.log(1e-1)))
    gdn_dt_bias = dt + jnp.log(-jnp.expm1(-dt))
    gdn_norm_w = gain((DEPTH, GDN_DV))
    diff_lambda_q1 = nrm((DEPTH, DIFF_DH), 0.1)
    diff_lambda_k1 = nrm((DEPTH, DIFF_DH), 0.1)
    diff_lambda_q2 = nrm((DEPTH, DIFF_DH), 0.1)
    diff_lambda_k2 = nrm((DEPTH, DIFF_DH), 0.1)
    diff_subln_w = gain((DEPTH, 2 * DIFF_DH))
    w_branch_sba = nrm((DEPTH, SBA_W, D_MODEL), SBA_W ** -0.5 * DEEPNORM_BETA)
    w_branch_gdn = nrm((DEPTH, GDN_W, D_MODEL), GDN_W ** -0.5 * DEEPNORM_BETA)
    w_branch_diff = nrm((DEPTH, DIFF_W, D_MODEL), DIFF_W ** -0.5 * DEEPNORM_BETA)
    w_out = nrm((DEPTH, D_MODEL, D_MODEL), D_MODEL ** -0.5 * DEEPNORM_BETA)
    ln1_g = gain((DEPTH, D_MODEL))
    ln1_b = nrm((DEPTH, D_MODEL), 0.02)
    w_router_group = nrm((DEPTH, D_MODEL, N_GROUPS), D_MODEL ** -0.5)
    b_router_group = nrm((DEPTH, N_GROUPS), 0.01)
    w_router_expert = nrm((DEPTH, D_MODEL, N_EXPERTS), D_MODEL ** -0.5)
    b_router_expert = nrm((DEPTH, N_EXPERTS), 0.01)
    w_expert_gate = nrm((DEPTH, N_EXPERTS, D_MODEL, D_EXPERT), D_MODEL ** -0.5)
    w_expert_up = nrm((DEPTH, N_EXPERTS, D_MODEL, D_EXPERT), D_MODEL ** -0.5)
    w_expert_down = nrm((DEPTH, N_EXPERTS, D_EXPERT, D_MODEL), D_EXPERT ** -0.5 * DEEPNORM_BETA)
    ln2_g = gain((DEPTH, D_MODEL))
    ln2_b = nrm((DEPTH, D_MODEL), 0.02)
    return {'x': x, 'w_in': w_in, 'conv_w': conv_w, 'gdn_a_log': gdn_a_log, 'gdn_dt_bias': gdn_dt_bias,
            'gdn_norm_w': gdn_norm_w, 'diff_lambda_q1': diff_lambda_q1, 'diff_lambda_k1': diff_lambda_k1,
            'diff_lambda_q2': diff_lambda_q2, 'diff_lambda_k2': diff_lambda_k2, 'diff_subln_w': diff_subln_w,
            'w_branch_sba': w_branch_sba, 'w_branch_gdn': w_branch_gdn, 'w_branch_diff': w_branch_diff,
            'w_out': w_out, 'ln1_g': ln1_g, 'ln1_b': ln1_b, 'w_router_group': w_router_group,
            'b_router_group': b_router_group, 'w_router_expert': w_router_expert,
            'b_router_expert': b_router_expert, 'w_expert_gate': w_expert_gate, 'w_expert_up': w_expert_up,
            'w_expert_down': w_expert_down, 'ln2_g': ln2_g, 'ln2_b': ln2_b}


def reference(x, w_in, conv_w, gdn_a_log, gdn_dt_bias, gdn_norm_w, diff_lambda_q1, diff_lambda_k1,
              diff_lambda_q2, diff_lambda_k2, diff_subln_w, w_branch_sba, w_branch_gdn, w_branch_diff,
              w_out, ln1_g, ln1_b, w_router_group, b_router_group, w_router_expert, b_router_expert,
              w_expert_gate, w_expert_up, w_expert_down, ln2_g, ln2_b):
    cos, sin = rope_tables(x.shape[1], DIFF_DH)
    for l in range(DEPTH):
        h = hybrid_mixer(x, w_in[l], conv_w[l], gdn_a_log[l], gdn_dt_bias[l], gdn_norm_w[l],
                         diff_lambda_q1[l], diff_lambda_k1[l], diff_lambda_q2[l], diff_lambda_k2[l],
                         diff_subln_w[l], w_branch_sba[l], w_branch_gdn[l], w_branch_diff[l], w_out[l],
                         cos, sin, l)
        x = layer_norm(DEEPNORM_ALPHA * x + h, ln1_g[l], ln1_b[l])
        h = hierarchical_moe(x, w_router_group[l], b_router_group[l], w_router_expert[l], b_router_expert[l],
                             w_expert_gate[l], w_expert_up[l], w_expert_down[l])
        x = layer_norm(DEEPNORM_ALPHA * x + h, ln2_g[l], ln2_b[l])
    return x
```

```python
import functools
import math

import jax
import jax.numpy as jnp
from jax import lax
from jax.experimental import pallas as pl
from jax.experimental.pallas import tpu as pltpu

_F32 = jnp.float32
_BF16 = jnp.bfloat16

_LANES = 128
_VMEM_LIMIT = 56 * 1024 * 1024
_HEAD_DIM = 128
_DIFF_DH = 64
_GDN_CHUNK = 64
_GDN_SUB = 16
_CONV_K = 4
_ROPE_THETA = 10000.0
_N_GROUPS = 4
_EXPERTS_PER_GROUP = 8
_N_EXPERTS = _N_GROUPS * _EXPERTS_PER_GROUP
_LN_EPS = 1e-5
_NORM_EPS = 1e-6
_NEG = -1e30


def _params(sem):
    return pltpu.CompilerParams(dimension_semantics=sem, vmem_limit_bytes=_VMEM_LIMIT)


def _nt_dot(a, b):
    return lax.dot_general(a, b, (((1,), (1,)), ((), ())), preferred_element_type=_F32)


def _dot(a, b):
    return jnp.dot(a, b, preferred_element_type=_F32)


def _split2(a):
    hi = a.astype(_BF16)
    lo = (a - hi.astype(_F32)).astype(_BF16)
    return hi, lo


def _dot_x3(a, b, nt=False):
    f = _nt_dot if nt else _dot
    ah, al = _split2(a)
    bh, bl = _split2(b)
    return f(ah, bh) + (f(ah, bl) + f(al, bh))


def _dot_exact_lhs(a_bf16, b, nt=False):
    f = _nt_dot if nt else _dot
    b1 = b.astype(_BF16)
    r1 = b - b1.astype(_F32)
    b2 = r1.astype(_BF16)
    b3 = (r1 - b2.astype(_F32)).astype(_BF16)
    return f(a_bf16, b1) + (f(a_bf16, b2) + f(a_bf16, b3))


def _sigmoid(x):
    return 1.0 / (1.0 + jnp.exp(-x))


def _softplus(x):
    return jnp.maximum(x, 0.0) + jnp.log(1.0 + jnp.exp(-jnp.abs(x)))


def _mm_kernel(x_ref, w_ref, o_ref):
    o_ref[...] = _dot(x_ref[...], w_ref[...]).astype(o_ref.dtype)


def _matmul(x, w, out_dtype, tm, tn, name):
    m, k = x.shape
    n = w.shape[1]
    tm, tn = min(tm, m), min(tn, n)
    return pl.pallas_call(
        _mm_kernel,
        out_shape=jax.ShapeDtypeStruct((m, n), out_dtype),
        grid=(m // tm, n // tn),
        in_specs=[pl.BlockSpec((tm, k), lambda i, j: (i, 0)),
                  pl.BlockSpec((k, tn), lambda i, j: (0, j))],
        out_specs=pl.BlockSpec((tm, tn), lambda i, j: (i, j)),
        compiler_params=_params(("parallel", "arbitrary")),
        name=name,
    )(x, w)


def _merge_kernel(ys_ref, yg_ref, yd_ref, ws_ref, wg_ref, wd_ref, gs_ref, gg_ref, gd_ref, o_ref):
    acc = _sigmoid(gs_ref[...].astype(_F32)) * _dot(ys_ref[...], ws_ref[...])
    acc += _sigmoid(gg_ref[...].astype(_F32)) * _dot(yg_ref[...], wg_ref[...])
    acc += _sigmoid(gd_ref[...].astype(_F32)) * _dot(yd_ref[...], wd_ref[...])
    o_ref[...] = acc.astype(o_ref.dtype)


def _branch_merge(y_sba, y_gdn, y_diff, wb_sba, wb_gdn, wb_diff, proj, gate_col0, tm, tn):
    t, w = y_sba.shape
    d = wb_sba.shape[1]
    tm, tn = min(tm, t), min(tn, d)
    g0 = gate_col0 // tn
    nd = d // tn
    y_spec = pl.BlockSpec((tm, w), lambda i, j: (i, 0))
    w_spec = pl.BlockSpec((w, tn), lambda i, j: (0, j))

    def gate_spec(b):
        return pl.BlockSpec((tm, tn), lambda i, j: (i, g0 + b * nd + j))

    return pl.pallas_call(
        _merge_kernel,
        out_shape=jax.ShapeDtypeStruct((t, d), _BF16),
        grid=(t // tm, nd),
        in_specs=[y_spec, y_spec, y_spec, w_spec, w_spec, w_spec,
                  gate_spec(0), gate_spec(1), gate_spec(2)],
        out_specs=pl.BlockSpec((tm, tn), lambda i, j: (i, j)),
        compiler_params=_params(("parallel", "arbitrary")),
        name="branch_merge",
    )(y_sba, y_gdn, y_diff, wb_sba, wb_gdn, wb_diff, proj, proj, proj)


def _outproj_kernel(m_ref, w_ref, x_ref, o_ref, *, alpha):
    o_ref[...] = alpha * x_ref[...] + _dot(m_ref[...], w_ref[...])


def _outproj_residual(merged, w_out, x, alpha, tm, tn):
    t, d = merged.shape
    tm, tn = min(tm, t), min(tn, d)
    return pl.pallas_call(
        functools.partial(_outproj_kernel, alpha=alpha),
        out_shape=jax.ShapeDtypeStruct((t, d), _F32),
        grid=(t // tm, d // tn),
        in_specs=[pl.BlockSpec((tm, d), lambda i, j: (i, 0)),
                  pl.BlockSpec((d, tn), lambda i, j: (0, j)),
                  pl.BlockSpec((tm, tn), lambda i, j: (i, j))],
        out_specs=pl.BlockSpec((tm, tn), lambda i, j: (i, j)),
        compiler_params=_params(("parallel", "arbitrary")),
        name="outproj_residual",
    )(merged, w_out, x)


def _layer_norm_rows(y, g, b):
    mu = jnp.mean(y, axis=-1, keepdims=True)
    yc = y - mu
    var = jnp.mean(yc * yc, axis=-1, keepdims=True)
    return yc * lax.rsqrt(var + _LN_EPS) * g + b


def _ln_kernel(y_ref, g_ref, b_ref, o32_ref, o16_ref):
    out = _layer_norm_rows(y_ref[...], g_ref[...], b_ref[...])
    o32_ref[...] = out
    o16_ref[...] = out.astype(_BF16)


def _layer_norm(y, g, b, tr):
    t, d = y.shape
    tr = min(tr, t)
    row = pl.BlockSpec((tr, d), lambda i: (i, 0))
    vec = pl.BlockSpec((1, d), lambda i: (0, 0))
    return pl.pallas_call(
        _ln_kernel,
        out_shape=(jax.ShapeDtypeStruct((t, d), _F32), jax.ShapeDtypeStruct((t, d), _BF16)),
        grid=(t // tr,),
        in_specs=[row, vec, vec],
        out_specs=(row, row),
        compiler_params=_params(("parallel",)),
        name="layer_norm",
    )(y, g.reshape(1, d), b.reshape(1, d))


def _sba_kernel(q_ref, k_ref, v_ref, u_ref, o_ref, acc_ref, *, tq, tk, scale):
    qi = pl.program_id(2)
    q = q_ref[...]
    u = u_ref[...]
    n_diag = tq // tk
    acc_ref[...] = jnp.zeros_like(acc_ref)

    def process(kb, carry, masked):
        ks = pl.multiple_of(kb * tk, tk)
        k = k_ref[pl.ds(ks, tk), :]
        v = v_ref[pl.ds(ks, tk), :]
        z = _nt_dot(q, k) * scale
        lneg = -_softplus(z)
        if masked:
            rows = qi * tq + lax.broadcasted_iota(jnp.int32, (tq, tk), 0)
            cols = kb * tk + lax.broadcasted_iota(jnp.int32, (tq, tk), 1)
            strict = cols < rows
            lneg = jnp.where(strict, lneg, 0.0)
        hi, lo = _split2(lneg)
        later = _dot(hi, u) + _dot(lo, u)
        att = jnp.exp(z + lneg + later + carry)
        if masked:
            att = jnp.where(strict, att, 0.0)
        acc_ref[...] += _dot(att.astype(_BF16), v)
        return carry + jnp.sum(lneg, axis=1, keepdims=True)

    carry = jnp.zeros((tq, 1), _F32)
    for d in reversed(range(n_diag)):
        carry = process(qi * n_diag + d, carry, True)
    n_full = qi * n_diag
    lax.fori_loop(0, n_full, lambda i, c: process(n_full - 1 - i, c, False), carry)
    o_ref[...] = acc_ref[...].astype(o_ref.dtype)


def _stick_breaking(proj, bsz, seq, heads, tq, tk):
    t = bsz * seq
    tq, tk = min(tq, seq), min(tk, seq)
    tk = min(tk, tq)
    nq = seq // tq
    ids = lax.broadcasted_iota(jnp.int32, (tk, tk), 0)
    u = (ids > ids.T).astype(_BF16)
    return pl.pallas_call(
        functools.partial(_sba_kernel, tq=tq, tk=tk, scale=_HEAD_DIM ** -0.5),
        out_shape=jax.ShapeDtypeStruct((t, heads * _HEAD_DIM), _BF16),
        grid=(bsz, heads, nq),
        in_specs=[pl.BlockSpec((tq, _HEAD_DIM), lambda b, h, i: (b * nq + i, h)),
                  pl.BlockSpec((seq, _HEAD_DIM), lambda b, h, i: (b, heads + h)),
                  pl.BlockSpec((seq, _HEAD_DIM), lambda b, h, i: (b, 2 * heads + h)),
                  pl.BlockSpec((tk, tk), lambda b, h, i: (0, 0))],
        out_specs=pl.BlockSpec((tq, _HEAD_DIM), lambda b, h, i: (b * nq + i, h)),
        scratch_shapes=[pltpu.VMEM((tq, _HEAD_DIM), _F32)],
        compiler_params=_params(("parallel", "parallel", "arbitrary")),
        name="stick_breaking_attention",
    )(proj, proj, proj, u)


def _rope_kernel(q_ref, k_ref, cos_ref, sa_ref, sb_ref, qo_ref, ko_ref, *, heads, scale):
    cos, sa, sb = cos_ref[...], sa_ref[...], sb_ref[...]
    for h in range(heads):
        sl = slice(h * _HEAD_DIM, (h + 1) * _HEAD_DIM)
        for src, dst, s in ((q_ref, qo_ref, scale), (k_ref, ko_ref, 1.0)):
            x = src[:, sl].astype(_F32)
            r = (x * cos + pltpu.roll(x, _HEAD_DIM - _DIFF_DH // 2, axis=1) * sa
                 + pltpu.roll(x, _DIFF_DH // 2, axis=1) * sb)
            dst[:, sl] = (r * s).astype(dst.dtype)


def _rope_tables(seq):
    half = _DIFF_DH // 2
    pos = jnp.arange(seq, dtype=_F32)
    inv_freq = _ROPE_THETA ** (-jnp.arange(0, _DIFF_DH, 2, dtype=_F32) / _DIFF_DH)
    ang = pos[:, None] * inv_freq[None, :]
    cos, sin, zero = jnp.cos(ang), jnp.sin(ang), jnp.zeros_like(ang)
    cos_t = jnp.concatenate([cos] * 4, axis=-1)
    sa_t = jnp.concatenate([-sin, zero] * 2, axis=-1)
    sb_t = jnp.concatenate([zero, sin] * 2, axis=-1)
    return cos_t, sa_t, sb_t


def _rope(proj, tables, bsz, seq, heads, ts):
    t = bsz * seq
    w = heads * _HEAD_DIM
    ts = min(ts, seq)
    ns = seq // ts
    tab = pl.BlockSpec((ts, _HEAD_DIM), lambda i: (i % ns, 0))
    out = pl.BlockSpec((ts, w), lambda i: (i, 0))
    return pl.pallas_call(
        functools.partial(_rope_kernel, heads=heads, scale=_DIFF_DH ** -0.5),
        out_shape=(jax.ShapeDtypeStruct((t, w), _BF16), jax.ShapeDtypeStruct((t, w), _BF16)),
        grid=(t // ts,),
        in_specs=[pl.BlockSpec((ts, w), lambda i: (i, 7)),
                  pl.BlockSpec((ts, w), lambda i: (i, 8)), tab, tab, tab],
        out_specs=(out, out),
        compiler_params=_params(("parallel",)),
        name="diff_rope",
    )(proj, proj, *tables)


def _diff_kernel(lam_ref, sw_ref, q_ref, k_ref, v_ref, o_ref, acc_ref, *, tq, tk, lam_init):
    qi = pl.program_id(2)
    q = q_ref[...]
    lane = lax.broadcasted_iota(jnp.int32, (tq, _HEAD_DIM), 1)
    zero = jnp.zeros_like(q)
    qs = (jnp.where(lane < _DIFF_DH, q, zero), jnp.where(lane >= _DIFF_DH, q, zero))
    n_diag = tq // tk
    acc_ref[...] = jnp.zeros_like(acc_ref)

    def process(kb, carry, masked):
        ks = pl.multiple_of(kb * tk, tk)
        k = k_ref[pl.ds(ks, tk), :]
        v = v_ref[pl.ds(ks, tk), :]
        if masked:
            rows = qi * tq + lax.broadcasted_iota(jnp.int32, (tq, tk), 0)
            cols = kb * tk + lax.broadcasted_iota(jnp.int32, (tq, tk), 1)
            causal = cols <= rows
        out = []
        for c in range(2):
            m_prev, l_prev = carry[2 * c], carry[2 * c + 1]
            s = _nt_dot(qs[c], k)
            if masked:
                s = jnp.where(causal, s, _NEG)
            m_new = jnp.maximum(m_prev, jnp.max(s, axis=1, keepdims=True))
            p = jnp.exp(s - m_new)
            alpha = jnp.exp(m_prev - m_new)
            acc_ref[c] = alpha * acc_ref[c] + _dot(p.astype(_BF16), v)
            out += [m_new, alpha * l_prev + jnp.sum(p, axis=1, keepdims=True)]
        return tuple(out)

    neg = jnp.full((tq, 1), _NEG, _F32)
    zcol = jnp.zeros((tq, 1), _F32)
    carry = (neg, zcol, neg, zcol)
    for d in reversed(range(n_diag)):
        carry = process(qi * n_diag + d, carry, True)
    n_full = qi * n_diag
    carry = lax.fori_loop(0, n_full, lambda i, c: process(n_full - 1 - i, c, False), carry)

    lv = lam_ref[...]
    lam = (jnp.exp(jnp.sum(lv[0:1] * lv[1:2], axis=1, keepdims=True))
           - jnp.exp(jnp.sum(lv[2:3] * lv[3:4], axis=1, keepdims=True)) + lam_init)
    o = acc_ref[0] / carry[1] - lam * (acc_ref[1] / carry[3])
    o = o * lax.rsqrt(jnp.mean(o * o, axis=1, keepdims=True) + _LN_EPS) * sw_ref[...]
    o_ref[...] = (o * (1.0 - lam_init)).astype(o_ref.dtype)


def _diff_attention(q_r, k_r, proj, lam_vecs, subln_w, lam_init, bsz, seq, heads, tq, tk):
    t = bsz * seq
    tq, tk = min(tq, seq), min(tk, seq)
    tk = min(tk, tq)
    nq = seq // tq
    return pl.pallas_call(
        functools.partial(_diff_kernel, tq=tq, tk=tk, lam_init=lam_init),
        out_shape=jax.ShapeDtypeStruct((t, heads * _HEAD_DIM), _BF16),
        grid=(bsz, heads, nq),
        in_specs=[pl.BlockSpec((4, _DIFF_DH), lambda b, h, i: (0, 0)),
                  pl.BlockSpec((1, _HEAD_DIM), lambda b, h, i: (0, 0)),
                  pl.BlockSpec((tq, _HEAD_DIM), lambda b, h, i: (b * nq + i, h)),
                  pl.BlockSpec((seq, _HEAD_DIM), lambda b, h, i: (b, h)),
                  pl.BlockSpec((seq, _HEAD_DIM), lambda b, h, i: (b, 9 * heads + h))],
        out_specs=pl.BlockSpec((tq, _HEAD_DIM), lambda b, h, i: (b * nq + i, h)),
        scratch_shapes=[pltpu.VMEM((2, tq, _HEAD_DIM), _F32)],
        compiler_params=_params(("parallel", "parallel", "arbitrary")),
        name="differential_attention",
    )(lam_vecs, subln_w.reshape(1, _HEAD_DIM), q_r, k_r, proj)


def _unit_lower_inverse(lm, eye, blockdiag):
    c = lm.shape[0]
    md = jnp.where(blockdiag, -lm, 0.0)
    x = eye + md
    p = _dot_x3(md, md)
    steps = int(math.log2(_GDN_SUB)) - 1
    for it in range(steps):
        x = x + _dot_x3(x, p)
        if it + 1 < steps:
            p = _dot_x3(p, p)
    n = _dot_x3(x, jnp.where(blockdiag, 0.0, lm))
    y = eye - n
    pw = n
    for _ in range(int(math.log2(c // _GDN_SUB)) - 1):
        pw = _dot_x3(pw, pw)
        y = y + _dot_x3(y, pw)
    return _dot_x3(y, x)


def _gdn_kernel(ab_ref, q_ref, k_ref, v_ref, z_ref, wq_ref, wk_ref, wv_ref, alog_ref, dtb_ref, nw_ref,
                tri_ref, o_ref, xp_ref, qs_ref, ks_ref, vs_ref, g_ref, beta_ref, state_ref,
                *, tb, gh, heads):
    hg = pl.program_id(1)
    i = pl.program_id(2)
    c = _GDN_CHUNK
    halo = 8

    @pl.when(i == 0)
    def _():
        xp_ref[:, 0:halo, :] = jnp.zeros((3, halo, gh * _HEAD_DIM), _F32)
        state_ref[...] = jnp.zeros_like(state_ref)

    @pl.when(i > 0)
    def _():
        xp_ref[:, 0:halo, :] = xp_ref[:, tb:tb + halo, :]

    for idx, (src, w_ref, dst) in enumerate(((q_ref, wq_ref, qs_ref), (k_ref, wk_ref, ks_ref),
                                             (v_ref, wv_ref, vs_ref))):
        xp_ref[idx, halo:halo + tb, :] = src[...].astype(_F32)
        w = w_ref[...]
        y = xp_ref[idx, halo - 3:halo - 3 + tb, :] * w[0:1, :]
        for j in range(1, _CONV_K):
            y = y + xp_ref[idx, halo - 3 + j:halo - 3 + j + tb, :] * w[j:j + 1, :]
        y = y * _sigmoid(y)
        for g in range(gh):
            sl = slice(g * _HEAD_DIM, (g + 1) * _HEAD_DIM)
            yh = y[:, sl]
            if idx < 2:
                yh = yh * lax.rsqrt(jnp.sum(yh * yh, axis=1, keepdims=True) + _NORM_EPS)
            if idx == 0:
                yh = yh * (_HEAD_DIM ** -0.5)
            dst[:, sl] = yh

    ab = ab_ref[...]
    g_ref[...] = -jnp.exp(alog_ref[...]) * _softplus(ab + dtb_ref[...])
    beta_ref[...] = _sigmoid(ab)

    lane = lax.broadcasted_iota(jnp.int32, (c, _LANES), 1)
    ri = lax.broadcasted_iota(jnp.int32, (c, c), 0)
    ci = lax.broadcasted_iota(jnp.int32, (c, c), 1)
    incl, strict = ri >= ci, ri > ci
    sub_shift = int(math.log2(_GDN_SUB))
    blockdiag = lax.shift_right_logical(ri, sub_shift) == lax.shift_right_logical(ci, sub_shift)
    eye = (ri == ci).astype(_F32)
    tri = tri_ref[...]
    nw = nw_ref[...]

    def chunk(cidx, _):
        r0 = pl.multiple_of(cidx * c, c)
        rows = pl.ds(r0, c)
        gcum_all = _dot_exact_lhs(tri, g_ref[rows, :])
        beta_all = beta_ref[rows, :]
        for g in range(gh):
            h = hg * gh + g
            sl = slice(g * _HEAD_DIM, (g + 1) * _HEAD_DIM)
            q, k, v = qs_ref[rows, sl], ks_ref[rows, sl], vs_ref[rows, sl]
            sel = lane == h
            gc = jnp.sum(jnp.where(sel, gcum_all, 0.0), axis=1, keepdims=True)
            beta = jnp.sum(jnp.where(lane == heads + h, beta_all, 0.0), axis=1, keepdims=True)
            g_row = _dot_exact_lhs(sel.astype(_BF16), gcum_all, nt=True)
            decay = jnp.where(incl, jnp.exp(jnp.where(incl, gc - g_row, 0.0)), 0.0)
            eg = jnp.exp(gc)
            g_last = gc[c - 1:c, :]
            kb = k * beta
            lm = jnp.where(strict, _nt_dot(kb.astype(_BF16), k.astype(_BF16)) * decay, 0.0)
            tinv = _unit_lower_inverse(lm, eye, blockdiag)
            u = _dot_x3(tinv, v * beta)
            w = _dot_x3(tinv, kb * eg)
            intra = _nt_dot(q.astype(_BF16), k.astype(_BF16)) * decay
            q_dec = (q * eg).astype(_BF16)
            k_dec = (k * jnp.exp(g_last - gc)).astype(_BF16)
            st = state_ref[g]
            st16 = st.astype(_BF16)
            v_new = u - _dot(w.astype(_BF16), st16)
            o = _dot(q_dec, st16) + _dot(intra.astype(_BF16), v_new.astype(_BF16))
            state_ref[g] = st * jnp.exp(g_last) + lax.dot_general(
                k_dec, v_new.astype(_BF16), (((0,), (0,)), ((), ())), preferred_element_type=_F32)
            zf = z_ref[rows, sl].astype(_F32)
            o = o * lax.rsqrt(jnp.mean(o * o, axis=1, keepdims=True) + _NORM_EPS) * nw
            o_ref[rows, sl] = (o * (zf * _sigmoid(zf))).astype(o_ref.dtype)
        return 0

    lax.fori_loop(0, tb // c, chunk, 0)


def _gated_deltanet(proj, ab, conv_w, a_log, dt_bias, norm_w, bsz, seq, heads, tb, gh):
    t = bsz * seq
    w = heads * _HEAD_DIM
    tb = min(tb, seq)
    gh = min(gh, heads)
    nb = seq // tb
    ng = heads // gh
    c = _GDN_CHUNK
    ids = lax.broadcasted_iota(jnp.int32, (c, c), 0)
    tri = (ids >= ids.T).astype(_BF16)
    pad = lambda vec: jnp.pad(vec.astype(_F32), (0, _LANES - heads)).reshape(1, _LANES)
    blk = (tb, gh * _HEAD_DIM)

    def col(base):
        return pl.BlockSpec(blk, lambda b, g, i: (b * nb + i, base * ng + g))

    def wcol(base):
        return pl.BlockSpec((_CONV_K, gh * _HEAD_DIM), lambda b, g, i: (0, base * ng + g))

    vec = pl.BlockSpec((1, _LANES), lambda b, g, i: (0, 0))
    return pl.pallas_call(
        functools.partial(_gdn_kernel, tb=tb, gh=gh, heads=heads),
        out_shape=jax.ShapeDtypeStruct((t, w), _BF16),
        grid=(bsz, ng, nb),
        in_specs=[pl.BlockSpec((tb, _LANES), lambda b, g, i: (b * nb + i, 0)),
                  col(3), col(4), col(5), col(6), wcol(0), wcol(1), wcol(2), vec, vec, vec,
                  pl.BlockSpec((c, c), lambda b, g, i: (0, 0))],
        out_specs=pl.BlockSpec(blk, lambda b, g, i: (b * nb + i, g)),
        scratch_shapes=[pltpu.VMEM((3, tb + 8, gh * _HEAD_DIM), _F32),
                        pltpu.VMEM(blk, _F32), pltpu.VMEM(blk, _F32), pltpu.VMEM(blk, _F32),
                        pltpu.VMEM((tb, _LANES), _F32), pltpu.VMEM((tb, _LANES), _F32),
                        pltpu.VMEM((gh, _HEAD_DIM, _HEAD_DIM), _F32)],
        compiler_params=_params(("parallel", "parallel", "arbitrary")),
        name="gated_deltanet",
    )(ab, proj, proj, proj, proj, conv_w, conv_w, conv_w, pad(a_log), pad(dt_bias),
      norm_w.reshape(1, _HEAD_DIM).astype(_F32), tri)


def _router_kernel(x_ref, w_ref, b_ref, tri_ref, route_ref, counts_ref, carry_ref, *, tm):
    i = pl.program_id(0)

    @pl.when(i == 0)
    def _():
        carry_ref[...] = jnp.zeros_like(carry_ref)

    logits = _dot_x3(x_ref[...], w_ref[...]) + b_ref[...]
    lane = lax.broadcasted_iota(jnp.int32, (tm, _LANES), 1)
    big = jnp.int32(_LANES)

    def top(vals):
        m = jnp.max(vals, axis=1, keepdims=True)
        idx = jnp.min(jnp.where(vals == m, lane, big), axis=1, keepdims=True)
        return m, idx

    gl = jnp.where(lane < _N_GROUPS, logits, _NEG)
    gm, gidx = top(gl)
    g_w = 1.0 / jnp.sum(jnp.exp(gl - gm), axis=1, keepdims=True)
    lane_group = lax.shift_right_arithmetic(lane - _N_GROUPS, int(math.log2(_EXPERTS_PER_GROUP)))
    el = jnp.where(lane_group == gidx, logits, _NEG)
    m1, i1 = top(el)
    m2, i2 = top(jnp.where(lane == i1, _NEG, el))
    r = jnp.exp(m2 - m1)
    w1 = g_w / (1.0 + r)
    w2 = g_w * r / (1.0 + r)
    e1, e2 = i1 - _N_GROUPS, i2 - _N_GROUPS
    oh1 = (lane == e1).astype(_F32)
    oh2 = (lane == e2).astype(_F32)
    both = oh1 + oh2
    before = _dot(tri_ref[...], both.astype(_BF16)) + carry_ref[0:1, :]
    rank1 = jnp.sum(oh1 * before, axis=1, keepdims=True)
    rank2 = jnp.sum(oh2 * before, axis=1, keepdims=True)
    out = jnp.zeros((tm, _LANES), _F32)
    for pos, val in enumerate((e1.astype(_F32), e2.astype(_F32), w1, w2, rank1, rank2)):
        out = jnp.where(lane == pos, val, out)
    route_ref[...] = out
    total = carry_ref[0:1, :] + jnp.sum(both, axis=0, keepdims=True)
    carry_ref[...] = jnp.broadcast_to(total, carry_ref.shape)
    counts_ref[...] = carry_ref[...]


def _router(x32, w_r, b_r, tm):
    t, d = x32.shape
    tm = min(tm, t)
    ids = lax.broadcasted_iota(jnp.int32, (tm, tm), 0)
    tri = (ids > ids.T).astype(_BF16)
    return pl.pallas_call(
        functools.partial(_router_kernel, tm=tm),
        out_shape=(jax.ShapeDtypeStruct((t, _LANES), _F32), jax.ShapeDtypeStruct((8, _LANES), _F32)),
        grid=(t // tm,),
        in_specs=[pl.BlockSpec((tm, d), lambda i: (i, 0)),
                  pl.BlockSpec((d, _LANES), lambda i: (0, 0)),
                  pl.BlockSpec((1, _LANES), lambda i: (0, 0)),
                  pl.BlockSpec((tm, tm), lambda i: (0, 0))],
        out_specs=(pl.BlockSpec((tm, _LANES), lambda i: (i, 0)),
                   pl.BlockSpec((8, _LANES), lambda i: (0, 0))),
        scratch_shapes=[pltpu.VMEM((8, _LANES), _F32)],
        compiler_params=_params(("arbitrary",)),
        name="moe_router",
    )(x32, w_r, b_r, tri)


def _row_copy(src_hbm, dst_hbm, sem, src_row, dst_row):
    return pltpu.make_async_copy(src_hbm.at[pl.ds(src_row, 1)], dst_hbm.at[pl.ds(dst_row, 1)], sem)


def _dispatch_kernel(tok_ref, x_hbm, xs_hbm, sem, *, tm):
    base = pl.program_id(0) * tm

    def issue(r, _):
        _row_copy(x_hbm, xs_hbm, sem, tok_ref[base + r], base + r).start()
        return 0

    lax.fori_loop(0, tm, issue, 0)

    def drain(r, _):
        _row_copy(x_hbm, xs_hbm, sem, 0, base + r).wait()
        return 0

    lax.fori_loop(0, tm, drain, 0)


def _dispatch(x32, tok_of_slot, n_tiles, tm):
    t, d = x32.shape
    return pl.pallas_call(
        functools.partial(_dispatch_kernel, tm=tm),
        out_shape=jax.ShapeDtypeStruct((n_tiles * tm, d), _F32),
        grid_spec=pltpu.PrefetchScalarGridSpec(
            num_scalar_prefetch=1,
            grid=(n_tiles,),
            in_specs=[pl.BlockSpec(memory_space=pl.ANY)],
            out_specs=pl.BlockSpec(memory_space=pl.ANY),
            scratch_shapes=[pltpu.SemaphoreType.DMA(())]),
        compiler_params=_params(("arbitrary",)),
        name="moe_dispatch",
    )(tok_of_slot, x32)


def _expert_kernel(te_ref, nact_ref, xs_ref, wg_ref, wu_ref, wd_ref, o_ref):
    active = pl.program_id(0) < nact_ref[0]

    @pl.when(active)
    def _():
        x = xs_ref[...].astype(_BF16)
        hg = _dot(x, wg_ref[0])
        hu = _dot(x, wu_ref[0])
        h = (hg * _sigmoid(hg)) * hu
        o_ref[...] = _dot(h.astype(_BF16), wd_ref[0])

    @pl.when(jnp.logical_not(active))
    def _():
        o_ref[...] = jnp.zeros_like(o_ref)


def _expert_ffn(xs, w_gate, w_up, w_down, tile_expert, nact, tm):
    p, d = xs.shape
    f = w_gate.shape[2]
    rows = pl.BlockSpec((tm, d), lambda i, te, na: (i, 0))
    return pl.pallas_call(
        _expert_kernel,
        out_shape=jax.ShapeDtypeStruct((p, d), _F32),
        grid_spec=pltpu.PrefetchScalarGridSpec(
            num_scalar_prefetch=2,
            grid=(p // tm,),
            in_specs=[rows,
                      pl.BlockSpec((1, d, f), lambda i, te, na: (te[i], 0, 0)),
                      pl.BlockSpec((1, d, f), lambda i, te, na: (te[i], 0, 0)),
                      pl.BlockSpec((1, f, d), lambda i, te, na: (te[i], 0, 0))],
            out_specs=rows),
        compiler_params=_params(("arbitrary",)),
        name="moe_expert_ffn",
    )(tile_expert, nact, xs, w_gate, w_up, w_down)


def _combine_kernel(pos_ref, ys_hbm, route_ref, x_ref, g_ref, b_ref, o32_ref, o16_ref, buf_ref, sem,
                    *, tc, alpha):
    base = pl.program_id(0) * tc

    def issue(r, _):
        for k in range(2):
            pltpu.make_async_copy(ys_hbm.at[pl.ds(pos_ref[2 * (base + r) + k], 1)],
                                  buf_ref.at[k, pl.ds(r, 1)], sem).start()
        return 0

    lax.fori_loop(0, tc, issue, 0)

    def drain(r, _):
        for k in range(2):
            pltpu.make_async_copy(ys_hbm.at[pl.ds(0, 1)], buf_ref.at[k, pl.ds(r, 1)], sem).wait()
        return 0

    lax.fori_loop(0, tc, drain, 0)
    route = route_ref[...]
    w1, w2 = route[:, 2:3], route[:, 3:4]
    y = alpha * x_ref[...] + (w1 * buf_ref[0] + w2 * buf_ref[1])
    out = _layer_norm_rows(y, g_ref[...], b_ref[...])
    o32_ref[...] = out
    o16_ref[...] = out.astype(_BF16)


def _combine_norm(ys, pos_flat, route, x32, g, b, alpha, tc):
    t, d = x32.shape
    tc = min(tc, t)
    row = lambda: pl.BlockSpec((tc, d), lambda i, pos: (i, 0))
    vec = lambda: pl.BlockSpec((1, d), lambda i, pos: (0, 0))
    return pl.pallas_call(
        functools.partial(_combine_kernel, tc=tc, alpha=alpha),
        out_shape=(jax.ShapeDtypeStruct((t, d), _F32), jax.ShapeDtypeStruct((t, d), _BF16)),
        grid_spec=pltpu.PrefetchScalarGridSpec(
            num_scalar_prefetch=1,
            grid=(t // tc,),
            in_specs=[pl.BlockSpec(memory_space=pl.ANY),
                      pl.BlockSpec((tc, _LANES), lambda i, pos: (i, 0)), row(), vec(), vec()],
            out_specs=(row(), row()),
            scratch_shapes=[pltpu.VMEM((2, tc, d), _F32), pltpu.SemaphoreType.DMA(())]),
        compiler_params=_params(("arbitrary",)),
        name="moe_combine_norm",
    )(pos_flat, ys, route, x32, g.reshape(1, d), b.reshape(1, d))


def _moe_layer(x32, w_rg, b_rg, w_re, b_re, w_gate, w_up, w_down, ln_g, ln_b, alpha, tm_r, tm_e, tc):
    t, d = x32.shape
    pad = _LANES - _N_GROUPS - _N_EXPERTS
    w_r = jnp.pad(jnp.concatenate([w_rg, w_re], axis=1).astype(_F32), ((0, 0), (0, pad)))
    b_r = jnp.pad(jnp.concatenate([b_rg, b_re]).astype(_F32), (0, pad)).reshape(1, _LANES)
    route, counts = _router(x32, w_r, b_r, tm_r)

    counts = counts[0, :_N_EXPERTS].astype(jnp.int32)
    padded = ((counts + tm_e - 1) // tm_e) * tm_e
    ends = jnp.cumsum(padded)
    starts = ends - padded
    eid = route[:, 0:2].astype(jnp.int32)
    pos = starts[eid] + route[:, 4:6].astype(jnp.int32)
    n_tiles = (2 * t) // tm_e + _N_EXPERTS
    nact = (ends[-1] // tm_e).astype(jnp.int32).reshape(1)
    tile_start = jnp.minimum(jnp.arange(n_tiles, dtype=jnp.int32), nact[0] - 1) * tm_e
    tile_expert = jnp.minimum(jnp.searchsorted(ends, tile_start, side="right"),
                              _N_EXPERTS - 1).astype(jnp.int32)
    tok = jnp.broadcast_to(jnp.arange(t, dtype=jnp.int32)[:, None], (t, 2))
    tok_of_slot = jnp.zeros((n_tiles * tm_e,), jnp.int32).at[pos.reshape(-1)].set(tok.reshape(-1))

    xs = _dispatch(x32, tok_of_slot, n_tiles, tm_e)
    ys = _expert_ffn(xs, w_gate, w_up, w_down, tile_expert, nact, tm_e)
    return _combine_norm(ys, pos.reshape(-1), route, x32, ln_g, ln_b, alpha, tc)


def kernel(x, w_in, conv_w, gdn_a_log, gdn_dt_bias, gdn_norm_w, diff_lambda_q1, diff_lambda_k1,
           diff_lambda_q2, diff_lambda_k2, diff_subln_w, w_branch_sba, w_branch_gdn, w_branch_diff,
           w_out, ln1_g, ln1_b, w_router_group, b_router_group, w_router_expert, b_router_expert,
           w_expert_gate, w_expert_up, w_expert_down, ln2_g, ln2_b):
    bsz, seq, d = x.shape
    depth = w_in.shape[0]
    t = bsz * seq
    heads = d // 256
    w = heads * _HEAD_DIM
    alpha = (2 * depth) ** 0.25
    big = t >= 8192
    tm_e = 256 if big else 64

    tables = _rope_tables(seq)
    ab0 = 3 * w + 4 * w
    ab1 = ab0 + 2 * heads

    x32 = x.reshape(t, d)
    x16 = x32.astype(_BF16)
    for l in range(depth):
        w_main = jnp.concatenate([w_in[l, :, :ab0], w_in[l, :, ab1:]], axis=1).astype(_BF16)
        w_ab = jnp.pad(w_in[l, :, ab0:ab1], ((0, 0), (0, _LANES - 2 * heads))).astype(_BF16)
        proj = _matmul(x16, w_main, _BF16, 1024, 512, "in_proj")
        ab = _matmul(x16, w_ab, _F32, 1024, _LANES, "in_proj_ab")

        y_sba = _stick_breaking(proj, bsz, seq, heads, 512, 256)
        y_gdn = _gated_deltanet(proj, ab, conv_w[l], gdn_a_log[l], gdn_dt_bias[l], gdn_norm_w[l],
                                bsz, seq, heads, 512, 4)
        q_r, k_r = _rope(proj, tables, bsz, seq, heads, 512)
        lam_vecs = jnp.stack([diff_lambda_q1[l], diff_lambda_k1[l], diff_lambda_q2[l],
                              diff_lambda_k2[l]]).astype(_F32)
        lam_init = 0.8 - 0.6 * math.exp(-0.3 * l)
        y_diff = _diff_attention(q_r, k_r, proj, lam_vecs, diff_subln_w[l].astype(_F32), lam_init,
                                 bsz, seq, heads, 512, 256)

        merged = _branch_merge(y_sba, y_gdn, y_diff, w_branch_sba[l].astype(_BF16),
                               w_branch_gdn[l].astype(_BF16), w_branch_diff[l].astype(_BF16),
                               proj, 10 * w, 512, 512)
        h = _outproj_residual(merged, w_out[l].astype(_BF16), x32, alpha, 1024, 512)
        x32, x16 = _layer_norm(h, ln1_g[l], ln1_b[l], 256)

        x32, x16 = _moe_layer(x32, w_router_group[l], b_router_group[l], w_router_expert[l],
                              b_router_expert[l], w_expert_gate[l].astype(_BF16),
                              w_expert_up[l].astype(_BF16), w_expert_down[l].astype(_BF16),
                              ln2_g[l], ln2_b[l], alpha, 512, tm_e, 128)
    return x32.reshape(bsz, seq, d)
```

```python
import functools
import math

import jax
import jax.numpy as jnp
from jax import lax
from jax.experimental import pallas as pl
from jax.experimental.pallas import tpu as pltpu

_F32 = jnp.float32
_BF16 = jnp.bfloat16

_LANES = 128
_VMEM_LIMIT = 56 * 1024 * 1024
_HEAD_DIM = 128
_DIFF_DH = 64
_GDN_CHUNK = 64
_GDN_SUB = 16
_CONV_K = 4
_ROPE_THETA = 10000.0
_N_GROUPS = 4
_EXPERTS_PER_GROUP = 8
_N_EXPERTS = _N_GROUPS * _EXPERTS_PER_GROUP
_LN_EPS = 1e-5
_NORM_EPS = 1e-6
_NEG = -1e30
_LOG2E = 1.4426950408889634


def _params(sem):
    return pltpu.CompilerParams(dimension_semantics=sem, vmem_limit_bytes=_VMEM_LIMIT)


def _nt_dot(a, b):
    return lax.dot_general(a, b, (((1,), (1,)), ((), ())), preferred_element_type=_F32)


def _dot(a, b):
    return jnp.dot(a, b, preferred_element_type=_F32)


def _split2(a):
    hi = a.astype(_BF16)
    lo = (a - hi.astype(_F32)).astype(_BF16)
    return hi, lo


def _dot_x3(a, b, nt=False):
    f = _nt_dot if nt else _dot
    ah, al = _split2(a)
    bh, bl = _split2(b)
    return f(ah, bh) + (f(ah, bl) + f(al, bh))


def _dot_exact_lhs(a_bf16, b, nt=False):
    f = _nt_dot if nt else _dot
    b1 = b.astype(_BF16)
    r1 = b - b1.astype(_F32)
    b2 = r1.astype(_BF16)
    b3 = (r1 - b2.astype(_F32)).astype(_BF16)
    return f(a_bf16, b1) + (f(a_bf16, b2) + f(a_bf16, b3))


def _sigmoid(x):
    return 1.0 / (1.0 + jnp.exp(-x))


def _softplus(x):
    return jnp.maximum(x, 0.0) + jnp.log(1.0 + jnp.exp(-jnp.abs(x)))


def _mm_kernel(x_ref, w_ref, o_ref):
    o_ref[...] = _dot(x_ref[...], w_ref[...]).astype(o_ref.dtype)


def _matmul(x, w, out_dtype, tm, tn, name):
    m, k = x.shape
    n = w.shape[1]
    tm, tn = min(tm, m), min(tn, n)
    return pl.pallas_call(
        _mm_kernel,
        out_shape=jax.ShapeDtypeStruct((m, n), out_dtype),
        grid=(m // tm, n // tn),
        in_specs=[pl.BlockSpec((tm, k), lambda i, j: (i, 0)),
                  pl.BlockSpec((k, tn), lambda i, j: (0, j))],
        out_specs=pl.BlockSpec((tm, tn), lambda i, j: (i, j)),
        compiler_params=_params(("parallel", "arbitrary")),
        name=name,
    )(x, w)


def _merge_kernel(ys_ref, yg_ref, yd_ref, ws_ref, wg_ref, wd_ref, gs_ref, gg_ref, gd_ref, o_ref):
    acc = _sigmoid(gs_ref[...].astype(_F32)) * _dot(ys_ref[...], ws_ref[...])
    acc += _sigmoid(gg_ref[...].astype(_F32)) * _dot(yg_ref[...], wg_ref[...])
    acc += _sigmoid(gd_ref[...].astype(_F32)) * _dot(yd_ref[...], wd_ref[...])
    o_ref[...] = acc.astype(o_ref.dtype)


def _branch_merge(y_sba, y_gdn, y_diff, wb_sba, wb_gdn, wb_diff, proj, gate_col0, tm, tn):
    t, w = y_sba.shape
    d = wb_sba.shape[1]
    tm, tn = min(tm, t), min(tn, d)
    g0 = gate_col0 // tn
    nd = d // tn
    y_spec = pl.BlockSpec((tm, w), lambda i, j: (i, 0))
    w_spec = pl.BlockSpec((w, tn), lambda i, j: (0, j))

    def gate_spec(b):
        return pl.BlockSpec((tm, tn), lambda i, j: (i, g0 + b * nd + j))

    return pl.pallas_call(
        _merge_kernel,
        out_shape=jax.ShapeDtypeStruct((t, d), _BF16),
        grid=(t // tm, nd),
        in_specs=[y_spec, y_spec, y_spec, w_spec, w_spec, w_spec,
                  gate_spec(0), gate_spec(1), gate_spec(2)],
        out_specs=pl.BlockSpec((tm, tn), lambda i, j: (i, j)),
        compiler_params=_params(("parallel", "arbitrary")),
        name="branch_merge",
    )(y_sba, y_gdn, y_diff, wb_sba, wb_gdn, wb_diff, proj, proj, proj)


def _outproj_kernel(m_ref, w_ref, x_ref, o_ref, *, alpha):
    o_ref[...] = alpha * x_ref[...] + _dot(m_ref[...], w_ref[...])


def _outproj_residual(merged, w_out, x, alpha, tm, tn):
    t, d = merged.shape
    tm, tn = min(tm, t), min(tn, d)
    return pl.pallas_call(
        functools.partial(_outproj_kernel, alpha=alpha),
        out_shape=jax.ShapeDtypeStruct((t, d), _F32),
        grid=(t // tm, d // tn),
        in_specs=[pl.BlockSpec((tm, d), lambda i, j: (i, 0)),
                  pl.BlockSpec((d, tn), lambda i, j: (0, j)),
                  pl.BlockSpec((tm, tn), lambda i, j: (i, j))],
        out_specs=pl.BlockSpec((tm, tn), lambda i, j: (i, j)),
        compiler_params=_params(("parallel", "arbitrary")),
        name="outproj_residual",
    )(merged, w_out, x)


def _layer_norm_rows(y, g, b):
    mu = jnp.mean(y, axis=-1, keepdims=True)
    yc = y - mu
    var = jnp.mean(yc * yc, axis=-1, keepdims=True)
    return yc * lax.rsqrt(var + _LN_EPS) * g + b


def _ln_kernel(y_ref, g_ref, b_ref, o32_ref, o16_ref):
    out = _layer_norm_rows(y_ref[...], g_ref[...], b_ref[...])
    o32_ref[...] = out
    o16_ref[...] = out.astype(_BF16)


def _layer_norm(y, g, b, tr):
    t, d = y.shape
    tr = min(tr, t)
    row = pl.BlockSpec((tr, d), lambda i: (i, 0))
    vec = pl.BlockSpec((1, d), lambda i: (0, 0))
    return pl.pallas_call(
        _ln_kernel,
        out_shape=(jax.ShapeDtypeStruct((t, d), _F32), jax.ShapeDtypeStruct((t, d), _BF16)),
        grid=(t // tr,),
        in_specs=[row, vec, vec],
        out_specs=(row, row),
        compiler_params=_params(("parallel",)),
        name="layer_norm",
    )(y, g.reshape(1, d), b.reshape(1, d))


def _neg_abs(x):
    bits = lax.bitcast_convert_type(x, jnp.uint32) | jnp.uint32(0x80000000)
    return lax.bitcast_convert_type(bits, _F32)


_SBA_UNROLL = 4


def _sba_kernel(q_ref, k_ref, v_ref, u_ref, o_ref, acc_ref, w_ref, *, tq, tk, scale):
    qi = pl.program_id(2)
    q = q_ref[...]
    u = u_ref[...]
    n_diag = tq // tk
    acc_ref[...] = jnp.zeros_like(acc_ref)

    def scores(kb):
        ks = pl.multiple_of(jnp.maximum(kb, 0) * tk, tk)
        return _nt_dot(q, k_ref[pl.ds(ks, tk), :]) * (scale * _LOG2E)

    def consume(w, kb, carry, masked):
        v = v_ref[pl.ds(pl.multiple_of(kb * tk, tk), tk), :]
        sp = jnp.maximum(w, 0.0) + jnp.log(1.0 + jnp.exp2(_neg_abs(w))) * _LOG2E
        if masked:
            rows = qi * tq + lax.broadcasted_iota(jnp.int32, (tq, tk), 0)
            cols = kb * tk + lax.broadcasted_iota(jnp.int32, (tq, tk), 1)
            strict = cols < rows
            sp = jnp.where(strict, sp, 0.0)
        later = _dot(sp.astype(_BF16), u)
        att = jnp.exp2(((w - sp) - later) - carry)
        if masked:
            att = jnp.where(strict, att, 0.0)
        acc_ref[...] += _dot(att.astype(_BF16), v)
        return carry + jnp.sum(sp, axis=1, keepdims=True)

    def run(kb0, carry, nblk, masked):
        for j in range(nblk):
            w_ref[(j + 1) % 2] = scores(kb0 - j - 1)
            carry = consume(w_ref[j % 2], kb0 - j, carry, masked)
        return carry

    top = (qi + 1) * n_diag - 1
    w_ref[0] = scores(top)
    carry = run(top, jnp.zeros((tq, 1), _F32), n_diag, True)
    n_full = qi * n_diag
    n_main = n_full // _SBA_UNROLL
    carry = lax.fori_loop(
        0, n_main, lambda i, c: run(n_full - 1 - _SBA_UNROLL * i, c, _SBA_UNROLL, False), carry)
    n_rem = (n_full - n_main * _SBA_UNROLL) // 2
    lax.fori_loop(
        0, n_rem, lambda i, c: run(n_full - 1 - _SBA_UNROLL * n_main - 2 * i, c, 2, False), carry)
    o_ref[...] = acc_ref[...].astype(o_ref.dtype)


def _stick_breaking(proj, bsz, seq, heads, tq, tk):
    t = bsz * seq
    tq, tk = min(tq, seq), min(tk, seq)
    tk = min(tk, tq)
    nq = seq // tq
    assert (tq // tk) % 2 == 0, "the two score slots alternate per key block"
    ids = lax.broadcasted_iota(jnp.int32, (tk, tk), 0)
    u = (ids > ids.T).astype(_BF16)
    return pl.pallas_call(
        functools.partial(_sba_kernel, tq=tq, tk=tk, scale=_HEAD_DIM ** -0.5),
        out_shape=jax.ShapeDtypeStruct((t, heads * _HEAD_DIM), _BF16),
        grid=(bsz, heads, nq),
        in_specs=[pl.BlockSpec((tq, _HEAD_DIM), lambda b, h, i: (b * nq + i, h)),
                  pl.BlockSpec((seq, _HEAD_DIM), lambda b, h, i: (b, heads + h)),
                  pl.BlockSpec((seq, _HEAD_DIM), lambda b, h, i: (b, 2 * heads + h)),
                  pl.BlockSpec((tk, tk), lambda b, h, i: (0, 0))],
        out_specs=pl.BlockSpec((tq, _HEAD_DIM), lambda b, h, i: (b * nq + i, h)),
        scratch_shapes=[pltpu.VMEM((tq, _HEAD_DIM), _F32), pltpu.VMEM((2, tq, tk), _F32)],
        compiler_params=_params(("parallel", "parallel", "arbitrary")),
        name="stick_breaking_attention",
    )(proj, proj, proj, u)


def _rope_kernel(q_ref, k_ref, cos_ref, sa_ref, sb_ref, qo_ref, ko_ref, *, heads, scale):
    cos, sa, sb = cos_ref[...], sa_ref[...], sb_ref[...]
    for h in range(heads):
        sl = slice(h * _HEAD_DIM, (h + 1) * _HEAD_DIM)
        for src, dst, s in ((q_ref, qo_ref, scale), (k_ref, ko_ref, 1.0)):
            x = src[:, sl].astype(_F32)
            r = (x * cos + pltpu.roll(x, _HEAD_DIM - _DIFF_DH // 2, axis=1) * sa
                 + pltpu.roll(x, _DIFF_DH // 2, axis=1) * sb)
            dst[:, sl] = (r * s).astype(dst.dtype)


def _rope_tables(seq):
    half = _DIFF_DH // 2
    pos = jnp.arange(seq, dtype=_F32)
    inv_freq = _ROPE_THETA ** (-jnp.arange(0, _DIFF_DH, 2, dtype=_F32) / _DIFF_DH)
    ang = pos[:, None] * inv_freq[None, :]
    cos, sin, zero = jnp.cos(ang), jnp.sin(ang), jnp.zeros_like(ang)
    cos_t = jnp.concatenate([cos] * 4, axis=-1)
    sa_t = jnp.concatenate([-sin, zero] * 2, axis=-1)
    sb_t = jnp.concatenate([zero, sin] * 2, axis=-1)
    return cos_t, sa_t, sb_t


def _rope(proj, tables, bsz, seq, heads, ts):
    t = bsz * seq
    w = heads * _HEAD_DIM
    ts = min(ts, seq)
    ns = seq // ts
    tab = pl.BlockSpec((ts, _HEAD_DIM), lambda i: (i % ns, 0))
    out = pl.BlockSpec((ts, w), lambda i: (i, 0))
    return pl.pallas_call(
        functools.partial(_rope_kernel, heads=heads, scale=_DIFF_DH ** -0.5 * _LOG2E),
        out_shape=(jax.ShapeDtypeStruct((t, w), _BF16), jax.ShapeDtypeStruct((t, w), _BF16)),
        grid=(t // ts,),
        in_specs=[pl.BlockSpec((ts, w), lambda i: (i, 7)),
                  pl.BlockSpec((ts, w), lambda i: (i, 8)), tab, tab, tab],
        out_specs=(out, out),
        compiler_params=_params(("parallel",)),
        name="diff_rope",
    )(proj, proj, *tables)


_DIFF_UNROLL = 4


def _diff_kernel(lam_ref, sw_ref, q_ref, k_ref, v_ref, o_ref, acc_ref, l_ref, s_ref, *, tq, tk, lam_init):
    qi = pl.program_id(2)
    q = q_ref[...]
    lane = lax.broadcasted_iota(jnp.int32, (tq, _HEAD_DIM), 1)
    zero = jnp.zeros_like(q)
    qs = (jnp.where(lane < _DIFF_DH, q, zero), jnp.where(lane >= _DIFF_DH, q, zero))
    n_diag = tq // tk
    acc_ref[...] = jnp.zeros_like(acc_ref)
    l_ref[...] = jnp.zeros_like(l_ref)

    def put_scores(slot, kb):
        ks = pl.multiple_of(jnp.maximum(kb, 0) * tk, tk)
        k = k_ref[pl.ds(ks, tk), :]
        for c in range(2):
            s_ref[slot, c] = _nt_dot(qs[c], k)

    def consume(slot, kb, carry, masked):
        v = v_ref[pl.ds(pl.multiple_of(kb * tk, tk), tk), :]
        if masked:
            rows = qi * tq + lax.broadcasted_iota(jnp.int32, (tq, tk), 0)
            cols = kb * tk + lax.broadcasted_iota(jnp.int32, (tq, tk), 1)
            causal = cols <= rows
        out = []
        for c in range(2):
            m_prev = carry[c]
            s = s_ref[slot, c]
            if masked:
                s = jnp.where(causal, s, _NEG)
            m_new = jnp.maximum(m_prev, jnp.max(s, axis=1, keepdims=True))
            p = jnp.exp2(s - m_new)
            alpha = jnp.exp2(m_prev - m_new)
            acc_ref[c] = alpha * acc_ref[c] + _dot(p.astype(_BF16), v)
            lanes = p[:, 0:_LANES]
            for j in range(1, tk // _LANES):
                lanes = lanes + p[:, j * _LANES:(j + 1) * _LANES]
            l_ref[c] = alpha * l_ref[c] + lanes
            out.append(m_new)
        return tuple(out)

    def run(kb0, carry, nblk, masked):
        for j in range(nblk):
            put_scores((j + 1) % 2, kb0 - j - 1)
            carry = consume(j % 2, kb0 - j, carry, masked)
        return carry

    neg = jnp.full((tq, 1), _NEG, _F32)
    top = (qi + 1) * n_diag - 1
    put_scores(0, top)
    carry = run(top, (neg, neg), n_diag, True)
    n_full = qi * n_diag
    n_main = n_full // _DIFF_UNROLL
    carry = lax.fori_loop(
        0, n_main, lambda i, c: run(n_full - 1 - _DIFF_UNROLL * i, c, _DIFF_UNROLL, False), carry)
    n_rem = (n_full - n_main * _DIFF_UNROLL) // 2
    lax.fori_loop(
        0, n_rem, lambda i, c: run(n_full - 1 - _DIFF_UNROLL * n_main - 2 * i, c, 2, False), carry)

    lv = lam_ref[...]
    lam = (jnp.exp(jnp.sum(lv[0:1] * lv[1:2], axis=1, keepdims=True))
           - jnp.exp(jnp.sum(lv[2:3] * lv[3:4], axis=1, keepdims=True)) + lam_init)
    l0 = jnp.sum(l_ref[0], axis=1, keepdims=True)
    l1 = jnp.sum(l_ref[1], axis=1, keepdims=True)
    o = acc_ref[0] / l0 - lam * (acc_ref[1] / l1)
    o = o * lax.rsqrt(jnp.mean(o * o, axis=1, keepdims=True) + _LN_EPS) * sw_ref[...]
    o_ref[...] = (o * (1.0 - lam_init)).astype(o_ref.dtype)


def _diff_attention(q_r, k_r, proj, lam_vecs, subln_w, lam_init, bsz, seq, heads, tq, tk):
    t = bsz * seq
    tq, tk = min(tq, seq), min(tk, seq)
    tk = min(tk, tq)
    nq = seq // tq
    return pl.pallas_call(
        functools.partial(_diff_kernel, tq=tq, tk=tk, lam_init=lam_init),
        out_shape=jax.ShapeDtypeStruct((t, heads * _HEAD_DIM), _BF16),
        grid=(bsz, heads, nq),
        in_specs=[pl.BlockSpec((4, _DIFF_DH), lambda b, h, i: (0, 0)),
                  pl.BlockSpec((1, _HEAD_DIM), lambda b, h, i: (0, 0)),
                  pl.BlockSpec((tq, _HEAD_DIM), lambda b, h, i: (b * nq + i, h)),
                  pl.BlockSpec((seq, _HEAD_DIM), lambda b, h, i: (b, h)),
                  pl.BlockSpec((seq, _HEAD_DIM), lambda b, h, i: (b, 9 * heads + h))],
        out_specs=pl.BlockSpec((tq, _HEAD_DIM), lambda b, h, i: (b * nq + i, h)),
        scratch_shapes=[pltpu.VMEM((2, tq, _HEAD_DIM), _F32), pltpu.VMEM((2, tq, _LANES), _F32),
                        pltpu.VMEM((2, 2, tq, tk), _F32)],
        compiler_params=_params(("parallel", "parallel", "arbitrary")),
        name="differential_attention",
    )(lam_vecs, subln_w.reshape(1, _HEAD_DIM), q_r, k_r, proj)


def _bdot(a, b):
    return lax.dot_general(a, b, (((2,), (1,)), ((0,), (0,))), preferred_element_type=_F32)


def _bdot_nt(a, b):
    return lax.dot_general(a, b, (((2,), (2,)), ((0,), (0,))), preferred_element_type=_F32)


def _bdot_tn(a, b):
    return lax.dot_general(a, b, (((1,), (1,)), ((0,), (0,))), preferred_element_type=_F32)


def _bdot_x3(a, b):
    ah, al = _split2(a)
    bh, bl = _split2(b)
    return _bdot(ah, bh) + (_bdot(ah, bl) + _bdot(al, bh))


def _unit_lower_inverse(lm, eye, blockdiag):
    c = lm.shape[-1]
    md = jnp.where(blockdiag, -lm, 0.0)
    x = eye + md
    p = _bdot_x3(md, md)
    steps = int(math.log2(_GDN_SUB)) - 1
    for it in range(steps):
        x = x + _bdot_x3(x, p)
        if it + 1 < steps:
            p = _bdot_x3(p, p)
    n = _bdot_x3(x, jnp.where(blockdiag, 0.0, lm))
    y = eye - n
    pw = n
    for _ in range(int(math.log2(c // _GDN_SUB)) - 1):
        pw = _bdot_x3(pw, pw)
        y = y + _bdot_x3(y, pw)
    return _bdot_x3(y, x)


def _gdn_kernel(ab_ref, q_ref, k_ref, v_ref, z_ref, wq_ref, wk_ref, wv_ref, alog_ref, dtb_ref, nw_ref,
                tri_ref, o_ref, xp_ref, qs_ref, ks_ref, vs_ref, dm_ref, gc_ref, bt_ref,
                u_ref, w_ref, qd_ref, kd_ref, in_ref, dl_ref, state_ref, *, tb, gh):
    i = pl.program_id(2)
    c = _GDN_CHUNK
    nc = tb // c
    halo = 8

    @pl.when(i == 0)
    def _():
        xp_ref[:, 0:halo, :] = jnp.zeros((3, halo, gh * _HEAD_DIM), _F32)
        state_ref[...] = jnp.zeros_like(state_ref)

    @pl.when(i > 0)
    def _():
        xp_ref[:, 0:halo, :] = xp_ref[:, tb:tb + halo, :]

    for idx, (src, cw_ref, dst) in enumerate(((q_ref, wq_ref, qs_ref), (k_ref, wk_ref, ks_ref),
                                              (v_ref, wv_ref, vs_ref))):
        xp_ref[idx, halo:halo + tb, :] = src[...].astype(_F32)
        w = cw_ref[...]
        y = xp_ref[idx, halo - 3:halo - 3 + tb, :] * w[0:1, :]
        for j in range(1, _CONV_K):
            y = y + xp_ref[idx, halo - 3 + j:halo - 3 + j + tb, :] * w[j:j + 1, :]
        y = y * _sigmoid(y)
        for g in range(gh):
            yh = y[:, g * _HEAD_DIM:(g + 1) * _HEAD_DIM]
            if idx < 2:
                yh = yh * lax.rsqrt(jnp.sum(yh * yh, axis=1, keepdims=True) + _NORM_EPS)
            if idx == 0:
                yh = yh * (_HEAD_DIM ** -0.5)
            for ci in range(nc):
                dst[ci * gh + g] = yh[ci * c:(ci + 1) * c, :]

    ab = ab_ref[0]
    g_all = -jnp.exp(alog_ref[0]) * _softplus(ab + dtb_ref[0])
    beta_all = _sigmoid(ab)
    tri = tri_ref[...]
    pick = (lax.broadcasted_iota(jnp.int32, (8, _LANES), 0)
            == lax.broadcasted_iota(jnp.int32, (8, _LANES), 1)).astype(_BF16)
    for ci in range(nc):
        rows = slice(ci * c, (ci + 1) * c)
        gcum = _dot_exact_lhs(tri, g_all[rows, :])
        gcum_t = _dot_exact_lhs(pick, gcum, nt=True)
        for g in range(gh):
            gcol = gcum[:, g:g + 1]
            dm_ref[ci * gh + g] = gcol - gcum_t[g:g + 1, :]
            gc_ref[ci * gh + g] = jnp.broadcast_to(gcol, (c, _LANES))
            bt_ref[ci * gh + g] = jnp.broadcast_to(beta_all[rows, gh + g:gh + g + 1], (c, _LANES))

    ri = lax.broadcasted_iota(jnp.int32, (c, c), 0)
    cj = lax.broadcasted_iota(jnp.int32, (c, c), 1)
    incl, strict = ri >= cj, ri > cj
    sub_shift = int(math.log2(_GDN_SUB))
    blockdiag = lax.shift_right_logical(ri, sub_shift) == lax.shift_right_logical(cj, sub_shift)
    eye = (ri == cj).astype(_F32)

    gc = gc_ref[...]
    beta = bt_ref[...]
    decay = jnp.where(incl, jnp.exp(jnp.where(incl, dm_ref[...], 0.0)), 0.0)
    g_last = gc[:, c - 1:c, :]
    eg = jnp.exp(gc)
    dl_ref[...] = jnp.exp(g_last)
    q, k, v = qs_ref[...], ks_ref[...], vs_ref[...]
    k16 = k.astype(_BF16)
    kb = k * beta
    lm = jnp.where(strict, _bdot_nt(kb.astype(_BF16), k16) * decay, 0.0)
    tinv = _unit_lower_inverse(lm, eye, blockdiag)
    u_ref[...] = _bdot_x3(tinv, v * beta)
    w_ref[...] = _bdot_x3(tinv, kb * eg).astype(_BF16)
    in_ref[...] = (_bdot_nt(q.astype(_BF16), k16) * decay).astype(_BF16)
    qd_ref[...] = (q * eg).astype(_BF16)
    kd_ref[...] = (k * jnp.exp(g_last - gc)).astype(_BF16)

    nw = nw_ref[...]

    def chunk(cidx, _):
        sl = pl.ds(cidx * gh, gh)
        st = state_ref[...]
        st16 = st.astype(_BF16)
        v_new = u_ref[sl] - _bdot(w_ref[sl], st16)
        vn16 = v_new.astype(_BF16)
        o = _bdot(qd_ref[sl], st16) + _bdot(in_ref[sl], vn16)
        state_ref[...] = st * dl_ref[sl] + _bdot_tn(kd_ref[sl], vn16)
        o = o * lax.rsqrt(jnp.mean(o * o, axis=2, keepdims=True) + _NORM_EPS) * nw
        rows = pl.ds(pl.multiple_of(cidx * c, c), c)
        for g in range(gh):
            lanes = slice(g * _HEAD_DIM, (g + 1) * _HEAD_DIM)
            zf = z_ref[rows, lanes].astype(_F32)
            o_ref[rows, lanes] = (o[g] * (zf * _sigmoid(zf))).astype(o_ref.dtype)
        return 0

    lax.fori_loop(0, nc, chunk, 0)


def _gated_deltanet(proj, ab, conv_w, a_log, dt_bias, norm_w, bsz, seq, heads, tb, gh):
    t = bsz * seq
    w = heads * _HEAD_DIM
    tb = min(tb, seq)
    gh = min(gh, heads)
    assert gh <= 8, "one sublane tile holds the transposed per-head cumulative gates"
    nb = seq // tb
    ng = heads // gh
    c = _GDN_CHUNK
    nbatch = (tb // c) * gh
    ids = lax.broadcasted_iota(jnp.int32, (c, c), 0)
    tri = (ids >= ids.T).astype(_BF16)

    def group(x):
        return jnp.moveaxis(x.reshape(x.shape[0], ng, gh), 1, 0)

    lane_pad = ((0, 0), (0, 0), (0, _LANES - 2 * gh))
    ab_g = jnp.pad(jnp.concatenate([group(ab[:, :heads]), group(ab[:, heads:2 * heads])], axis=2), lane_pad)
    zeros = jnp.zeros((1, heads), _F32)
    alog_g = jnp.pad(jnp.concatenate([group(a_log.astype(_F32)[None]), group(zeros)], axis=2), lane_pad)
    dtb_g = jnp.pad(jnp.concatenate([group(dt_bias.astype(_F32)[None]), group(zeros)], axis=2), lane_pad)
    blk = (tb, gh * _HEAD_DIM)

    def col(base):
        return pl.BlockSpec(blk, lambda b, g, i: (b * nb + i, base * ng + g))

    def wcol(base):
        return pl.BlockSpec((_CONV_K, gh * _HEAD_DIM), lambda b, g, i: (0, base * ng + g))

    vec = pl.BlockSpec((1, 1, _LANES), lambda b, g, i: (g, 0, 0))
    batch_f32 = pltpu.VMEM((nbatch, c, _HEAD_DIM), _F32)
    batch_b16 = pltpu.VMEM((nbatch, c, _HEAD_DIM), _BF16)
    return pl.pallas_call(
        functools.partial(_gdn_kernel, tb=tb, gh=gh),
        out_shape=jax.ShapeDtypeStruct((t, w), _BF16),
        grid=(bsz, ng, nb),
        in_specs=[pl.BlockSpec((1, tb, _LANES), lambda b, g, i: (g, b * nb + i, 0)),
                  col(3), col(4), col(5), col(6), wcol(0), wcol(1), wcol(2), vec, vec,
                  pl.BlockSpec((1, _HEAD_DIM), lambda b, g, i: (0, 0)),
                  pl.BlockSpec((c, c), lambda b, g, i: (0, 0))],
        out_specs=pl.BlockSpec(blk, lambda b, g, i: (b * nb + i, g)),
        scratch_shapes=[pltpu.VMEM((3, tb + 8, gh * _HEAD_DIM), _F32),
                        batch_f32, batch_f32, batch_f32,
                        pltpu.VMEM((nbatch, c, c), _F32), batch_f32, batch_f32,
                        batch_f32, batch_b16, batch_b16, batch_b16,
                        pltpu.VMEM((nbatch, c, c), _BF16),
                        pltpu.VMEM((nbatch, 1, _LANES), _F32),
                        pltpu.VMEM((gh, _HEAD_DIM, _HEAD_DIM), _F32)],
        compiler_params=_params(("parallel", "parallel", "arbitrary")),
        name="gated_deltanet",
    )(ab_g, proj, proj, proj, proj, conv_w, conv_w, conv_w, alog_g, dtb_g,
      norm_w.reshape(1, _HEAD_DIM).astype(_F32), tri)


def _router_kernel(x_ref, w_ref, b_ref, tri_ref, route_ref, counts_ref, carry_ref, *, tm):
    i = pl.program_id(0)

    @pl.when(i == 0)
    def _():
        carry_ref[...] = jnp.zeros_like(carry_ref)

    logits = _dot_x3(x_ref[...], w_ref[...]) + b_ref[...]
    lane = lax.broadcasted_iota(jnp.int32, (tm, _LANES), 1)
    big = jnp.int32(_LANES)

    def top(vals):
        m = jnp.max(vals, axis=1, keepdims=True)
        idx = jnp.min(jnp.where(vals == m, lane, big), axis=1, keepdims=True)
        return m, idx

    gl = jnp.where(lane < _N_GROUPS, logits, _NEG)
    gm, gidx = top(gl)
    g_w = 1.0 / jnp.sum(jnp.exp(gl - gm), axis=1, keepdims=True)
    lane_group = lax.shift_right_arithmetic(lane - _N_GROUPS, int(math.log2(_EXPERTS_PER_GROUP)))
    el = jnp.where(lane_group == gidx, logits, _NEG)
    m1, i1 = top(el)
    m2, i2 = top(jnp.where(lane == i1, _NEG, el))
    r = jnp.exp(m2 - m1)
    w1 = g_w / (1.0 + r)
    w2 = g_w * r / (1.0 + r)
    e1, e2 = i1 - _N_GROUPS, i2 - _N_GROUPS
    oh1 = (lane == e1).astype(_F32)
    oh2 = (lane == e2).astype(_F32)
    both = oh1 + oh2
    before = _dot(tri_ref[...], both.astype(_BF16)) + carry_ref[0:1, :]
    rank1 = jnp.sum(oh1 * before, axis=1, keepdims=True)
    rank2 = jnp.sum(oh2 * before, axis=1, keepdims=True)
    out = jnp.zeros((tm, _LANES), _F32)
    for pos, val in enumerate((e1.astype(_F32), e2.astype(_F32), w1, w2, rank1, rank2)):
        out = jnp.where(lane == pos, val, out)
    route_ref[...] = out
    total = carry_ref[0:1, :] + jnp.sum(both, axis=0, keepdims=True)
    carry_ref[...] = jnp.broadcast_to(total, carry_ref.shape)
    counts_ref[...] = carry_ref[...]


def _router(x32, w_r, b_r, tm):
    t, d = x32.shape
    tm = min(tm, t)
    ids = lax.broadcasted_iota(jnp.int32, (tm, tm), 0)
    tri = (ids > ids.T).astype(_BF16)
    return pl.pallas_call(
        functools.partial(_router_kernel, tm=tm),
        out_shape=(jax.ShapeDtypeStruct((t, _LANES), _F32), jax.ShapeDtypeStruct((8, _LANES), _F32)),
        grid=(t // tm,),
        in_specs=[pl.BlockSpec((tm, d), lambda i: (i, 0)),
                  pl.BlockSpec((d, _LANES), lambda i: (0, 0)),
                  pl.BlockSpec((1, _LANES), lambda i: (0, 0)),
                  pl.BlockSpec((tm, tm), lambda i: (0, 0))],
        out_specs=(pl.BlockSpec((tm, _LANES), lambda i: (i, 0)),
                   pl.BlockSpec((8, _LANES), lambda i: (0, 0))),
        scratch_shapes=[pltpu.VMEM((8, _LANES), _F32)],
        compiler_params=_params(("arbitrary",)),
        name="moe_router",
    )(x32, w_r, b_r, tri)


_GATHER_UNROLL = 8


def _gather_rows(src_hbm, row_of, dst_ref, sem, n):
    def issue(r, _):
        pltpu.make_async_copy(src_hbm.at[pl.ds(row_of(r), 1)], dst_ref.at[pl.ds(r, 1)], sem).start()
        return 0

    lax.fori_loop(0, n, issue, 0, unroll=_GATHER_UNROLL)


def _wait_rows(src_hbm, dst_ref, sem, n):
    pltpu.make_async_copy(src_hbm.at[pl.ds(0, n)], dst_ref, sem).wait()


def _expert_kernel(te_ref, nact_ref, tok_ref, x_hbm, wg_ref, wu_ref, wd_ref, o_ref, buf_ref, sems, *, tm):
    i = pl.program_id(0)
    nact = nact_ref[0]
    slot = lax.rem(i, 2)

    def start(tile, s):
        base = tile * tm
        _gather_rows(x_hbm, lambda r: tok_ref[base + r], buf_ref.at[s], sems.at[s], tm)

    @pl.when(i == 0)
    def _():
        start(0, 0)

    @pl.when(i + 1 < nact)
    def _():
        start(i + 1, 1 - slot)

    @pl.when(i < nact)
    def _():
        _wait_rows(x_hbm, buf_ref.at[slot], sems.at[slot], tm)
        x = buf_ref[slot].astype(_BF16)
        hg = _dot(x, wg_ref[0])
        hu = _dot(x, wu_ref[0])
        h = (hg * _sigmoid(hg)) * hu
        o_ref[...] = _dot(h.astype(_BF16), wd_ref[0])

    @pl.when(i >= nact)
    def _():
        o_ref[...] = jnp.zeros_like(o_ref)


def _expert_ffn(x32, tok_of_slot, w_gate, w_up, w_down, tile_expert, nact, n_tiles, tm):
    t, d = x32.shape
    f = w_gate.shape[2]
    return pl.pallas_call(
        functools.partial(_expert_kernel, tm=tm),
        out_shape=jax.ShapeDtypeStruct((n_tiles * tm, d), _F32),
        grid_spec=pltpu.PrefetchScalarGridSpec(
            num_scalar_prefetch=3,
            grid=(n_tiles,),
            in_specs=[pl.BlockSpec(memory_space=pl.ANY),
                      pl.BlockSpec((1, d, f), lambda i, te, na, tok: (te[i], 0, 0)),
                      pl.BlockSpec((1, d, f), lambda i, te, na, tok: (te[i], 0, 0)),
                      pl.BlockSpec((1, f, d), lambda i, te, na, tok: (te[i], 0, 0))],
            out_specs=pl.BlockSpec((tm, d), lambda i, te, na, tok: (i, 0)),
            scratch_shapes=[pltpu.VMEM((2, tm, d), _F32), pltpu.SemaphoreType.DMA((2,))]),
        compiler_params=_params(("arbitrary",)),
        name="moe_expert_ffn",
    )(tile_expert, nact, tok_of_slot, x32, w_gate, w_up, w_down)


def _combine_kernel(pos_ref, ys_hbm, route_ref, x_ref, g_ref, b_ref, o32_ref, o16_ref, buf_ref, sems,
                    *, tc, alpha):
    base = pl.program_id(0) * tc
    for k in range(2):
        _gather_rows(ys_hbm, lambda r, k=k: pos_ref[2 * (base + r) + k], buf_ref.at[k], sems.at[k], tc)
    for k in range(2):
        _wait_rows(ys_hbm, buf_ref.at[k], sems.at[k], tc)
    route = route_ref[...]
    w1, w2 = route[:, 2:3], route[:, 3:4]
    y = alpha * x_ref[...] + (w1 * buf_ref[0] + w2 * buf_ref[1])
    out = _layer_norm_rows(y, g_ref[...], b_ref[...])
    o32_ref[...] = out
    o16_ref[...] = out.astype(_BF16)


def _combine_norm(ys, pos_flat, route, x32, g, b, alpha, tc):
    t, d = x32.shape
    tc = min(tc, t)
    row = lambda: pl.BlockSpec((tc, d), lambda i, pos: (i, 0))
    vec = lambda: pl.BlockSpec((1, d), lambda i, pos: (0, 0))
    return pl.pallas_call(
        functools.partial(_combine_kernel, tc=tc, alpha=alpha),
        out_shape=(jax.ShapeDtypeStruct((t, d), _F32), jax.ShapeDtypeStruct((t, d), _BF16)),
        grid_spec=pltpu.PrefetchScalarGridSpec(
            num_scalar_prefetch=1,
            grid=(t // tc,),
            in_specs=[pl.BlockSpec(memory_space=pl.ANY),
                      pl.BlockSpec((tc, _LANES), lambda i, pos: (i, 0)), row(), vec(), vec()],
            out_specs=(row(), row()),
            scratch_shapes=[pltpu.VMEM((2, tc, d), _F32), pltpu.SemaphoreType.DMA((2,))]),
        compiler_params=_params(("arbitrary",)),
        name="moe_combine_norm",
    )(pos_flat, ys, route, x32, g.reshape(1, d), b.reshape(1, d))


def _moe_layer(x32, w_rg, b_rg, w_re, b_re, w_gate, w_up, w_down, ln_g, ln_b, alpha, tm_r, tm_e, tc):
    t, d = x32.shape
    pad = _LANES - _N_GROUPS - _N_EXPERTS
    w_r = jnp.pad(jnp.concatenate([w_rg, w_re], axis=1).astype(_F32), ((0, 0), (0, pad)))
    b_r = jnp.pad(jnp.concatenate([b_rg, b_re]).astype(_F32), (0, pad)).reshape(1, _LANES)
    route, counts = _router(x32, w_r, b_r, tm_r)

    counts = counts[0, :_N_EXPERTS].astype(jnp.int32)
    padded = ((counts + tm_e - 1) // tm_e) * tm_e
    ends = jnp.cumsum(padded)
    starts = ends - padded
    eid = route[:, 0:2].astype(jnp.int32)
    pos = starts[eid] + route[:, 4:6].astype(jnp.int32)
    n_tiles = (2 * t) // tm_e + _N_EXPERTS
    nact = (ends[-1] // tm_e).astype(jnp.int32).reshape(1)
    tile_start = jnp.minimum(jnp.arange(n_tiles, dtype=jnp.int32), nact[0] - 1) * tm_e
    tile_expert = jnp.minimum(jnp.sum((ends[None, :] <= tile_start[:, None]).astype(jnp.int32), axis=1),
                              _N_EXPERTS - 1)
    tok = jnp.broadcast_to(jnp.arange(t, dtype=jnp.int32)[:, None], (t, 2))
    tok_of_slot = jnp.zeros((n_tiles * tm_e,), jnp.int32).at[pos.reshape(-1)].set(tok.reshape(-1))

    ys = _expert_ffn(x32, tok_of_slot, w_gate, w_up, w_down, tile_expert, nact, n_tiles, tm_e)
    return _combine_norm(ys, pos.reshape(-1), route, x32, ln_g, ln_b, alpha, tc)


def kernel(x, w_in, conv_w, gdn_a_log, gdn_dt_bias, gdn_norm_w, diff_lambda_q1, diff_lambda_k1,
           diff_lambda_q2, diff_lambda_k2, diff_subln_w, w_branch_sba, w_branch_gdn, w_branch_diff,
           w_out, ln1_g, ln1_b, w_router_group, b_router_group, w_router_expert, b_router_expert,
           w_expert_gate, w_expert_up, w_expert_down, ln2_g, ln2_b):
    bsz, seq, d = x.shape
    depth = w_in.shape[0]
    t = bsz * seq
    heads = d // 256
    w = heads * _HEAD_DIM
    alpha = (2 * depth) ** 0.25
    big = t >= 8192
    tm_e = 256 if big else 64

    tables = _rope_tables(seq)
    ab0 = 3 * w + 4 * w
    ab1 = ab0 + 2 * heads

    x32 = x.reshape(t, d)
    x16 = x32.astype(_BF16)
    for l in range(depth):
        w_main = jnp.concatenate([w_in[l, :, :ab0], w_in[l, :, ab1:]], axis=1).astype(_BF16)
        w_ab = jnp.pad(w_in[l, :, ab0:ab1], ((0, 0), (0, _LANES - 2 * heads))).astype(_BF16)
        proj = _matmul(x16, w_main, _BF16, 1024, 512, "in_proj")
        ab = _matmul(x16, w_ab, _F32, 1024, _LANES, "in_proj_ab")

        y_sba = _stick_breaking(proj, bsz, seq, heads, 512, 256)
        y_gdn = _gated_deltanet(proj, ab, conv_w[l], gdn_a_log[l], gdn_dt_bias[l], gdn_norm_w[l],
                                bsz, seq, heads, 512, 4)
        q_r, k_r = _rope(proj, tables, bsz, seq, heads, 512)
        lam_vecs = jnp.stack([diff_lambda_q1[l], diff_lambda_k1[l], diff_lambda_q2[l],
                              diff_lambda_k2[l]]).astype(_F32)
        lam_init = 0.8 - 0.6 * math.exp(-0.3 * l)
        y_diff = _diff_attention(q_r, k_r, proj, lam_vecs, diff_subln_w[l].astype(_F32), lam_init,
                                 bsz, seq, heads, 512, 256)

        merged = _branch_merge(y_sba, y_gdn, y_diff, w_branch_sba[l].astype(_BF16),
                               w_branch_gdn[l].astype(_BF16), w_branch_diff[l].astype(_BF16),
                               proj, 10 * w, 512, 512)
        h = _outproj_residual(merged, w_out[l].astype(_BF16), x32, alpha, 1024, 512)
        x32, x16 = _layer_norm(h, ln1_g[l], ln1_b[l], 256)

        x32, x16 = _moe_layer(x32, w_router_group[l], b_router_group[l], w_router_expert[l],
                              b_router_expert[l], w_expert_gate[l].astype(_BF16),
                              w_expert_up[l].astype(_BF16), w_expert_down[l].astype(_BF16),
                              ln2_g[l], ln2_b[l], alpha, 512, tm_e, 128)
    return x32.reshape(bsz, seq, d)
```

```python
import functools
import math

import jax
import jax.numpy as jnp
from jax import lax
from jax.experimental import pallas as pl
from jax.experimental.pallas import tpu as pltpu

_F32 = jnp.float32
_BF16 = jnp.bfloat16

_LANES = 128
_VMEM_LIMIT = 56 * 1024 * 1024
_HEAD_DIM = 128
_DIFF_DH = 64
_GDN_CHUNK = 64
_GDN_SUB = 16
_CONV_K = 4
_ROPE_THETA = 10000.0
_N_GROUPS = 4
_EXPERTS_PER_GROUP = 8
_N_EXPERTS = _N_GROUPS * _EXPERTS_PER_GROUP
_LN_EPS = 1e-5
_NORM_EPS = 1e-6
_NEG = -1e30
_LOG2E = 1.4426950408889634


def _params(sem):
    return pltpu.CompilerParams(dimension_semantics=sem, vmem_limit_bytes=_VMEM_LIMIT)


def _nt_dot(a, b):
    return lax.dot_general(a, b, (((1,), (1,)), ((), ())), preferred_element_type=_F32)


def _dot(a, b):
    return jnp.dot(a, b, preferred_element_type=_F32)


def _split2(a):
    hi = a.astype(_BF16)
    lo = (a - hi.astype(_F32)).astype(_BF16)
    return hi, lo


def _dot_x3(a, b, nt=False):
    f = _nt_dot if nt else _dot
    ah, al = _split2(a)
    bh, bl = _split2(b)
    return f(ah, bh) + (f(ah, bl) + f(al, bh))


def _dot_exact_lhs(a_bf16, b, nt=False):
    f = _nt_dot if nt else _dot
    b1 = b.astype(_BF16)
    r1 = b - b1.astype(_F32)
    b2 = r1.astype(_BF16)
    b3 = (r1 - b2.astype(_F32)).astype(_BF16)
    return f(a_bf16, b1) + (f(a_bf16, b2) + f(a_bf16, b3))


def _sigmoid(x):
    return 1.0 / (1.0 + jnp.exp(-x))


def _softplus(x):
    return jnp.maximum(x, 0.0) + jnp.log(1.0 + jnp.exp(-jnp.abs(x)))


def _mm_kernel(x_ref, w_ref, o_ref):
    o_ref[...] = _dot(x_ref[...], w_ref[...]).astype(o_ref.dtype)


def _matmul(x, w, layer, out_dtype, tm, tn, name):
    m, k = x.shape
    n = w.shape[2]
    tm, tn = min(tm, m), min(tn, n)
    return pl.pallas_call(
        _mm_kernel,
        out_shape=jax.ShapeDtypeStruct((m, n), out_dtype),
        grid=(m // tm, n // tn),
        in_specs=[pl.BlockSpec((tm, k), lambda i, j: (i, 0)),
                  pl.BlockSpec((None, k, tn), lambda i, j: (layer, 0, j))],
        out_specs=pl.BlockSpec((tm, tn), lambda i, j: (i, j)),
        compiler_params=_params(("parallel", "arbitrary")),
        name=name,
    )(x, w)


def _merge_kernel(ys_ref, yg_ref, yd_ref, ws_ref, wg_ref, wd_ref, gs_ref, gg_ref, gd_ref, o_ref):
    acc = _sigmoid(gs_ref[...].astype(_F32)) * _dot(ys_ref[...], ws_ref[...])
    acc += _sigmoid(gg_ref[...].astype(_F32)) * _dot(yg_ref[...], wg_ref[...])
    acc += _sigmoid(gd_ref[...].astype(_F32)) * _dot(yd_ref[...], wd_ref[...])
    o_ref[...] = acc.astype(o_ref.dtype)


def _branch_merge(y_sba, y_gdn, y_diff, wb_sba, wb_gdn, wb_diff, layer, proj, gate_col0, tm, tn):
    t, w = y_sba.shape
    d = wb_sba.shape[2]
    tm, tn = min(tm, t), min(tn, d)
    g0 = gate_col0 // tn
    nd = d // tn
    y_spec = pl.BlockSpec((tm, w), lambda i, j: (i, 0))
    w_spec = pl.BlockSpec((None, w, tn), lambda i, j: (layer, 0, j))

    def gate_spec(b):
        return pl.BlockSpec((tm, tn), lambda i, j: (i, g0 + b * nd + j))

    return pl.pallas_call(
        _merge_kernel,
        out_shape=jax.ShapeDtypeStruct((t, d), _BF16),
        grid=(t // tm, nd),
        in_specs=[y_spec, y_spec, y_spec, w_spec, w_spec, w_spec,
                  gate_spec(0), gate_spec(1), gate_spec(2)],
        out_specs=pl.BlockSpec((tm, tn), lambda i, j: (i, j)),
        compiler_params=_params(("parallel", "arbitrary")),
        name="branch_merge",
    )(y_sba, y_gdn, y_diff, wb_sba, wb_gdn, wb_diff, proj, proj, proj)


def _outproj_kernel(m_ref, w_ref, x_ref, o_ref, *, alpha):
    o_ref[...] = alpha * x_ref[...] + _dot(m_ref[...], w_ref[...])


def _outproj_residual(merged, w_out, layer, x, alpha, tm, tn):
    t, d = merged.shape
    tm, tn = min(tm, t), min(tn, d)
    return pl.pallas_call(
        functools.partial(_outproj_kernel, alpha=alpha),
        out_shape=jax.ShapeDtypeStruct((t, d), _F32),
        grid=(t // tm, d // tn),
        in_specs=[pl.BlockSpec((tm, d), lambda i, j: (i, 0)),
                  pl.BlockSpec((None, d, tn), lambda i, j: (layer, 0, j)),
                  pl.BlockSpec((tm, tn), lambda i, j: (i, j))],
        out_specs=pl.BlockSpec((tm, tn), lambda i, j: (i, j)),
        compiler_params=_params(("parallel", "arbitrary")),
        name="outproj_residual",
    )(merged, w_out, x)


def _layer_norm_rows(y, g, b):
    mu = jnp.mean(y, axis=-1, keepdims=True)
    yc = y - mu
    var = jnp.mean(yc * yc, axis=-1, keepdims=True)
    return yc * lax.rsqrt(var + _LN_EPS) * g + b


def _ln_kernel(y_ref, g_ref, b_ref, o32_ref, o16_ref):
    out = _layer_norm_rows(y_ref[...], g_ref[...], b_ref[...])
    o32_ref[...] = out
    o16_ref[...] = out.astype(_BF16)


def _layer_norm(y, g, b, tr):
    t, d = y.shape
    tr = min(tr, t)
    row = pl.BlockSpec((tr, d), lambda i: (i, 0))
    vec = pl.BlockSpec((1, d), lambda i: (0, 0))
    return pl.pallas_call(
        _ln_kernel,
        out_shape=(jax.ShapeDtypeStruct((t, d), _F32), jax.ShapeDtypeStruct((t, d), _BF16)),
        grid=(t // tr,),
        in_specs=[row, vec, vec],
        out_specs=(row, row),
        compiler_params=_params(("parallel",)),
        name="layer_norm",
    )(y, g.reshape(1, d), b.reshape(1, d))


def _neg_abs(x):
    bits = lax.bitcast_convert_type(x, jnp.uint32) | jnp.uint32(0x80000000)
    return lax.bitcast_convert_type(bits, _F32)


_SBA_UNROLL = 4


def _sba_kernel(q_ref, k_ref, v_ref, u_ref, o_ref, acc_ref, w_ref, *, tq, tk, scale):
    qi = pl.program_id(2)
    q = (q_ref[...].astype(_F32) * (scale * _LOG2E)).astype(_BF16)
    u = u_ref[...]
    n_diag = tq // tk
    acc_ref[...] = jnp.zeros_like(acc_ref)

    def scores(kb):
        ks = pl.multiple_of(jnp.maximum(kb, 0) * tk, tk)
        return _nt_dot(q, k_ref[pl.ds(ks, tk), :])

    def consume(w, kb, carry, masked):
        v = v_ref[pl.ds(pl.multiple_of(kb * tk, tk), tk), :]
        sp = jnp.maximum(w, 0.0) + jnp.log(1.0 + jnp.exp2(_neg_abs(w))) * _LOG2E
        if masked:
            rows = qi * tq + lax.broadcasted_iota(jnp.int32, (tq, tk), 0)
            cols = kb * tk + lax.broadcasted_iota(jnp.int32, (tq, tk), 1)
            strict = cols < rows
            sp = jnp.where(strict, sp, 0.0)
        later = _dot(sp.astype(_BF16), u)
        att = jnp.exp2(((w - sp) - later) - carry)
        if masked:
            att = jnp.where(strict, att, 0.0)
        acc_ref[...] += _dot(att.astype(_BF16), v)
        return carry + jnp.sum(sp, axis=1, keepdims=True)

    def run(kb0, carry, nblk, masked):
        for j in range(nblk):
            w_ref[(j + 1) % 2] = scores(kb0 - j - 1)
            carry = consume(w_ref[j % 2], kb0 - j, carry, masked)
        return carry

    top = (qi + 1) * n_diag - 1
    w_ref[0] = scores(top)
    carry = run(top, jnp.zeros((tq, 1), _F32), n_diag, True)
    n_full = qi * n_diag
    n_main = n_full // _SBA_UNROLL
    carry = lax.fori_loop(
        0, n_main, lambda i, c: run(n_full - 1 - _SBA_UNROLL * i, c, _SBA_UNROLL, False), carry)
    n_rem = (n_full - n_main * _SBA_UNROLL) // 2
    lax.fori_loop(
        0, n_rem, lambda i, c: run(n_full - 1 - _SBA_UNROLL * n_main - 2 * i, c, 2, False), carry)
    o_ref[...] = acc_ref[...].astype(o_ref.dtype)


def _stick_breaking(proj, bsz, seq, heads, tq, tk):
    t = bsz * seq
    tq, tk = min(tq, seq), min(tk, seq)
    tk = min(tk, tq)
    nq = seq // tq
    assert (tq // tk) % 2 == 0, "the two score slots alternate per key block"
    ids = lax.broadcasted_iota(jnp.int32, (tk, tk), 0)
    u = (ids > ids.T).astype(_BF16)
    return pl.pallas_call(
        functools.partial(_sba_kernel, tq=tq, tk=tk, scale=_HEAD_DIM ** -0.5),
        out_shape=jax.ShapeDtypeStruct((t, heads * _HEAD_DIM), _BF16),
        grid=(bsz, heads, nq),
        in_specs=[pl.BlockSpec((tq, _HEAD_DIM), lambda b, h, i: (b * nq + i, h)),
                  pl.BlockSpec((seq, _HEAD_DIM), lambda b, h, i: (b, heads + h)),
                  pl.BlockSpec((seq, _HEAD_DIM), lambda b, h, i: (b, 2 * heads + h)),
                  pl.BlockSpec((tk, tk), lambda b, h, i: (0, 0))],
        out_specs=pl.BlockSpec((tq, _HEAD_DIM), lambda b, h, i: (b * nq + i, h)),
        scratch_shapes=[pltpu.VMEM((tq, _HEAD_DIM), _F32), pltpu.VMEM((2, tq, tk), _F32)],
        compiler_params=_params(("parallel", "parallel", "arbitrary")),
        name="stick_breaking_attention",
    )(proj, proj, proj, u)


def _rope_kernel(q_ref, k_ref, cos_ref, sa_ref, sb_ref, qo_ref, ko_ref, *, heads, scale):
    cos, sa, sb = cos_ref[...], sa_ref[...], sb_ref[...]
    for h in range(heads):
        sl = slice(h * _HEAD_DIM, (h + 1) * _HEAD_DIM)
        for src, dst, s in ((q_ref, qo_ref, scale), (k_ref, ko_ref, 1.0)):
            x = src[:, sl].astype(_F32)
            r = (x * cos + pltpu.roll(x, _HEAD_DIM - _DIFF_DH // 2, axis=1) * sa
                 + pltpu.roll(x, _DIFF_DH // 2, axis=1) * sb)
            dst[:, sl] = (r * s).astype(dst.dtype)


def _rope_tables(seq):
    half = _DIFF_DH // 2
    pos = jnp.arange(seq, dtype=_F32)
    inv_freq = _ROPE_THETA ** (-jnp.arange(0, _DIFF_DH, 2, dtype=_F32) / _DIFF_DH)
    ang = pos[:, None] * inv_freq[None, :]
    cos, sin, zero = jnp.cos(ang), jnp.sin(ang), jnp.zeros_like(ang)
    cos_t = jnp.concatenate([cos] * 4, axis=-1)
    sa_t = jnp.concatenate([-sin, zero] * 2, axis=-1)
    sb_t = jnp.concatenate([zero, sin] * 2, axis=-1)
    return cos_t, sa_t, sb_t


def _rope(proj, tables, bsz, seq, heads, ts):
    t = bsz * seq
    w = heads * _HEAD_DIM
    ts = min(ts, seq)
    ns = seq // ts
    tab = pl.BlockSpec((ts, _HEAD_DIM), lambda i: (i % ns, 0))
    out = pl.BlockSpec((ts, w), lambda i: (i, 0))
    return pl.pallas_call(
        functools.partial(_rope_kernel, heads=heads, scale=_DIFF_DH ** -0.5 * _LOG2E),
        out_shape=(jax.ShapeDtypeStruct((t, w), _BF16), jax.ShapeDtypeStruct((t, w), _BF16)),
        grid=(t // ts,),
        in_specs=[pl.BlockSpec((ts, w), lambda i: (i, 0)),
                  pl.BlockSpec((ts, w), lambda i: (i, 1)), tab, tab, tab],
        out_specs=(out, out),
        compiler_params=_params(("parallel",)),
        name="diff_rope",
    )(proj, proj, *tables)


_DIFF_UNROLL = 4


def _diff_kernel(lam_ref, sw_ref, q_ref, k_ref, v_ref, o_ref, acc_ref, l_ref, s_ref, *, tq, tk, lam_init):
    qi = pl.program_id(2)
    q = q_ref[...]
    lane = lax.broadcasted_iota(jnp.int32, (tq, _HEAD_DIM), 1)
    zero = jnp.zeros_like(q)
    qs = (jnp.where(lane < _DIFF_DH, q, zero), jnp.where(lane >= _DIFF_DH, q, zero))
    n_diag = tq // tk
    acc_ref[...] = jnp.zeros_like(acc_ref)
    l_ref[...] = jnp.zeros_like(l_ref)

    def put_scores(slot, kb):
        ks = pl.multiple_of(jnp.maximum(kb, 0) * tk, tk)
        k = k_ref[pl.ds(ks, tk), :]
        for c in range(2):
            s_ref[slot, c] = _nt_dot(qs[c], k)

    def consume(slot, kb, carry, masked):
        v = v_ref[pl.ds(pl.multiple_of(kb * tk, tk), tk), :]
        if masked:
            rows = qi * tq + lax.broadcasted_iota(jnp.int32, (tq, tk), 0)
            cols = kb * tk + lax.broadcasted_iota(jnp.int32, (tq, tk), 1)
            causal = cols <= rows
        out = []
        for c in range(2):
            m_prev = carry[c]
            s = s_ref[slot, c]
            if masked:
                s = jnp.where(causal, s, _NEG)
            m_new = jnp.maximum(m_prev, jnp.max(s, axis=1, keepdims=True))
            p = jnp.exp2(s - m_new)
            alpha = jnp.exp2(m_prev - m_new)
            acc_ref[c] = alpha * acc_ref[c] + _dot(p.astype(_BF16), v)
            lanes = p[:, 0:_LANES]
            for j in range(1, tk // _LANES):
                lanes = lanes + p[:, j * _LANES:(j + 1) * _LANES]
            l_ref[c] = alpha * l_ref[c] + lanes
            out.append(m_new)
        return tuple(out)

    def run(kb0, carry, nblk, masked):
        for j in range(nblk):
            put_scores((j + 1) % 2, kb0 - j - 1)
            carry = consume(j % 2, kb0 - j, carry, masked)
        return carry

    neg = jnp.full((tq, 1), _NEG, _F32)
    top = (qi + 1) * n_diag - 1
    put_scores(0, top)
    carry = run(top, (neg, neg), n_diag, True)
    n_full = qi * n_diag
    n_main = n_full // _DIFF_UNROLL
    carry = lax.fori_loop(
        0, n_main, lambda i, c: run(n_full - 1 - _DIFF_UNROLL * i, c, _DIFF_UNROLL, False), carry)
    n_rem = (n_full - n_main * _DIFF_UNROLL) // 2
    lax.fori_loop(
        0, n_rem, lambda i, c: run(n_full - 1 - _DIFF_UNROLL * n_main - 2 * i, c, 2, False), carry)

    lv = lam_ref[...]
    lam = (jnp.exp(jnp.sum(lv[0:1] * lv[1:2], axis=1, keepdims=True))
           - jnp.exp(jnp.sum(lv[2:3] * lv[3:4], axis=1, keepdims=True)) + lam_init)
    l0 = jnp.sum(l_ref[0], axis=1, keepdims=True)
    l1 = jnp.sum(l_ref[1], axis=1, keepdims=True)
    o = acc_ref[0] / l0 - lam * (acc_ref[1] / l1)
    o = o * lax.rsqrt(jnp.mean(o * o, axis=1, keepdims=True) + _LN_EPS) * sw_ref[...]
    o_ref[...] = (o * (1.0 - lam_init)).astype(o_ref.dtype)


def _diff_attention(q_r, k_r, proj, lam_vecs, subln_w, lam_init, bsz, seq, heads, tq, tk):
    t = bsz * seq
    tq, tk = min(tq, seq), min(tk, seq)
    tk = min(tk, tq)
    nq = seq // tq
    return pl.pallas_call(
        functools.partial(_diff_kernel, tq=tq, tk=tk, lam_init=lam_init),
        out_shape=jax.ShapeDtypeStruct((t, heads * _HEAD_DIM), _BF16),
        grid=(bsz, heads, nq),
        in_specs=[pl.BlockSpec((4, _DIFF_DH), lambda b, h, i: (0, 0)),
                  pl.BlockSpec((1, _HEAD_DIM), lambda b, h, i: (0, 0)),
                  pl.BlockSpec((tq, _HEAD_DIM), lambda b, h, i: (b * nq + i, h)),
                  pl.BlockSpec((seq, _HEAD_DIM), lambda b, h, i: (b, h)),
                  pl.BlockSpec((seq, _HEAD_DIM), lambda b, h, i: (b, 2 * heads + h))],
        out_specs=pl.BlockSpec((tq, _HEAD_DIM), lambda b, h, i: (b * nq + i, h)),
        scratch_shapes=[pltpu.VMEM((2, tq, _HEAD_DIM), _F32), pltpu.VMEM((2, tq, _LANES), _F32),
                        pltpu.VMEM((2, 2, tq, tk), _F32)],
        compiler_params=_params(("parallel", "parallel", "arbitrary")),
        name="differential_attention",
    )(lam_vecs, subln_w.reshape(1, _HEAD_DIM), q_r, k_r, proj)


def _bdot(a, b):
    return lax.dot_general(a, b, (((2,), (1,)), ((0,), (0,))), preferred_element_type=_F32)


def _bdot_nt(a, b):
    return lax.dot_general(a, b, (((2,), (2,)), ((0,), (0,))), preferred_element_type=_F32)


def _bdot_tn(a, b):
    return lax.dot_general(a, b, (((1,), (1,)), ((0,), (0,))), preferred_element_type=_F32)


def _bdot_x3(a, b):
    ah, al = _split2(a)
    bh, bl = _split2(b)
    return _bdot(ah, bh) + (_bdot(ah, bl) + _bdot(al, bh))


def _unit_lower_inverse(lm, eye, blockdiag):
    c = lm.shape[-1]
    md = jnp.where(blockdiag, -lm, 0.0)
    x = eye + md
    p = _bdot_x3(md, md)
    steps = int(math.log2(_GDN_SUB)) - 1
    for it in range(steps):
        x = x + _bdot_x3(x, p)
        if it + 1 < steps:
            p = _bdot_x3(p, p)
    n = _bdot_x3(x, jnp.where(blockdiag, 0.0, lm))
    y = eye - n
    pw = n
    for _ in range(int(math.log2(c // _GDN_SUB)) - 1):
        pw = _bdot_x3(pw, pw)
        y = y + _bdot_x3(y, pw)
    return _bdot_x3(y, x)


def _gdn_kernel(ab_ref, q_ref, k_ref, v_ref, z_ref, wq_ref, wk_ref, wv_ref, alog_ref, dtb_ref, nw_ref,
                tri_ref, o_ref, xp_ref, qs_ref, ks_ref, vs_ref, dm_ref, gc_ref, bt_ref,
                u_ref, w_ref, qd_ref, kd_ref, in_ref, dl_ref, state_ref, *, tb, gh):
    i = pl.program_id(2)
    c = _GDN_CHUNK
    nc = tb // c
    halo = 8

    @pl.when(i == 0)
    def _():
        xp_ref[:, 0:halo, :] = jnp.zeros((3, halo, gh * _HEAD_DIM), _F32)
        state_ref[...] = jnp.zeros_like(state_ref)

    @pl.when(i > 0)
    def _():
        xp_ref[:, 0:halo, :] = xp_ref[:, tb:tb + halo, :]

    for idx, (src, cw_ref, dst) in enumerate(((q_ref, wq_ref, qs_ref), (k_ref, wk_ref, ks_ref),
                                              (v_ref, wv_ref, vs_ref))):
        xp_ref[idx, halo:halo + tb, :] = src[...].astype(_F32)
        w = cw_ref[...]
        y = xp_ref[idx, halo - 3:halo - 3 + tb, :] * w[0:1, :]
        for j in range(1, _CONV_K):
            y = y + xp_ref[idx, halo - 3 + j:halo - 3 + j + tb, :] * w[j:j + 1, :]
        y = y * _sigmoid(y)
        for g in range(gh):
            yh = y[:, g * _HEAD_DIM:(g + 1) * _HEAD_DIM]
            if idx < 2:
                yh = yh * lax.rsqrt(jnp.sum(yh * yh, axis=1, keepdims=True) + _NORM_EPS)
            if idx == 0:
                yh = yh * (_HEAD_DIM ** -0.5)
            for ci in range(nc):
                dst[ci * gh + g] = yh[ci * c:(ci + 1) * c, :]

    ab = ab_ref[0]
    g_all = -jnp.exp(alog_ref[0]) * _softplus(ab + dtb_ref[0])
    beta_all = _sigmoid(ab)
    tri = tri_ref[...]
    pick = (lax.broadcasted_iota(jnp.int32, (8, _LANES), 0)
            == lax.broadcasted_iota(jnp.int32, (8, _LANES), 1)).astype(_BF16)
    for ci in range(nc):
        rows = slice(ci * c, (ci + 1) * c)
        gcum = _dot_exact_lhs(tri, g_all[rows, :])
        gcum_t = _dot_exact_lhs(pick, gcum, nt=True)
        for g in range(gh):
            gcol = gcum[:, g:g + 1]
            dm_ref[ci * gh + g] = gcol - gcum_t[g:g + 1, :]
            gc_ref[ci * gh + g] = jnp.broadcast_to(gcol, (c, _LANES))
            bt_ref[ci * gh + g] = jnp.broadcast_to(beta_all[rows, gh + g:gh + g + 1], (c, _LANES))

    ri = lax.broadcasted_iota(jnp.int32, (c, c), 0)
    cj = lax.broadcasted_iota(jnp.int32, (c, c), 1)
    incl, strict = ri >= cj, ri > cj
    sub_shift = int(math.log2(_GDN_SUB))
    blockdiag = lax.shift_right_logical(ri, sub_shift) == lax.shift_right_logical(cj, sub_shift)
    eye = (ri == cj).astype(_F32)

    gc = gc_ref[...]
    beta = bt_ref[...]
    decay = jnp.where(incl, jnp.exp(jnp.where(incl, dm_ref[...], 0.0)), 0.0)
    g_last = gc[:, c - 1:c, :]
    eg = jnp.exp(gc)
    dl_ref[...] = jnp.exp(g_last)
    q, k, v = qs_ref[...], ks_ref[...], vs_ref[...]
    k16 = k.astype(_BF16)
    kb = k * beta
    lm = jnp.where(strict, _bdot_nt(kb.astype(_BF16), k16) * decay, 0.0)
    tinv = _unit_lower_inverse(lm, eye, blockdiag)
    u_ref[...] = _bdot_x3(tinv, v * beta)
    w_ref[...] = _bdot_x3(tinv, kb * eg).astype(_BF16)
    in_ref[...] = (_bdot_nt(q.astype(_BF16), k16) * decay).astype(_BF16)
    qd_ref[...] = (q * eg).astype(_BF16)
    kd_ref[...] = (k * jnp.exp(g_last - gc)).astype(_BF16)

    nw = nw_ref[...]

    def chunk(cidx, _):
        sl = pl.ds(cidx * gh, gh)
        st = state_ref[...]
        st16 = st.astype(_BF16)
        v_new = u_ref[sl] - _bdot(w_ref[sl], st16)
        vn16 = v_new.astype(_BF16)
        o = _bdot(qd_ref[sl], st16) + _bdot(in_ref[sl], vn16)
        state_ref[...] = st * dl_ref[sl] + _bdot_tn(kd_ref[sl], vn16)
        o = o * lax.rsqrt(jnp.mean(o * o, axis=2, keepdims=True) + _NORM_EPS) * nw
        rows = pl.ds(pl.multiple_of(cidx * c, c), c)
        for g in range(gh):
            lanes = slice(g * _HEAD_DIM, (g + 1) * _HEAD_DIM)
            zf = z_ref[rows, lanes].astype(_F32)
            o_ref[rows, lanes] = (o[g] * (zf * _sigmoid(zf))).astype(o_ref.dtype)
        return 0

    lax.fori_loop(0, nc, chunk, 0)


def _gated_deltanet(proj, ab, conv_w, a_log, dt_bias, norm_w, bsz, seq, heads, tb, gh):
    t = bsz * seq
    w = heads * _HEAD_DIM
    tb = min(tb, seq)
    gh = min(gh, heads)
    assert gh <= 8, "one sublane tile holds the transposed per-head cumulative gates"
    nb = seq // tb
    ng = heads // gh
    c = _GDN_CHUNK
    nbatch = (tb // c) * gh
    ids = lax.broadcasted_iota(jnp.int32, (c, c), 0)
    tri = (ids >= ids.T).astype(_BF16)

    def group(x):
        return jnp.moveaxis(x.reshape(x.shape[0], ng, gh), 1, 0)

    lane_pad = ((0, 0), (0, 0), (0, _LANES - 2 * gh))
    ab_g = jnp.pad(jnp.concatenate([group(ab[:, :heads]), group(ab[:, heads:2 * heads])], axis=2), lane_pad)
    zeros = jnp.zeros((1, heads), _F32)
    alog_g = jnp.pad(jnp.concatenate([group(a_log.astype(_F32)[None]), group(zeros)], axis=2), lane_pad)
    dtb_g = jnp.pad(jnp.concatenate([group(dt_bias.astype(_F32)[None]), group(zeros)], axis=2), lane_pad)
    blk = (tb, gh * _HEAD_DIM)

    def col(base):
        return pl.BlockSpec(blk, lambda b, g, i: (b * nb + i, base * ng + g))

    def wcol(base):
        return pl.BlockSpec((_CONV_K, gh * _HEAD_DIM), lambda b, g, i: (0, base * ng + g))

    vec = pl.BlockSpec((1, 1, _LANES), lambda b, g, i: (g, 0, 0))
    batch_f32 = pltpu.VMEM((nbatch, c, _HEAD_DIM), _F32)
    batch_b16 = pltpu.VMEM((nbatch, c, _HEAD_DIM), _BF16)
    return pl.pallas_call(
        functools.partial(_gdn_kernel, tb=tb, gh=gh),
        out_shape=jax.ShapeDtypeStruct((t, w), _BF16),
        grid=(bsz, ng, nb),
        in_specs=[pl.BlockSpec((1, tb, _LANES), lambda b, g, i: (g, b * nb + i, 0)),
                  col(3), col(4), col(5), col(6), wcol(0), wcol(1), wcol(2), vec, vec,
                  pl.BlockSpec((1, _HEAD_DIM), lambda b, g, i: (0, 0)),
                  pl.BlockSpec((c, c), lambda b, g, i: (0, 0))],
        out_specs=pl.BlockSpec(blk, lambda b, g, i: (b * nb + i, g)),
        scratch_shapes=[pltpu.VMEM((3, tb + 8, gh * _HEAD_DIM), _F32),
                        batch_f32, batch_f32, batch_f32,
                        pltpu.VMEM((nbatch, c, c), _F32), batch_f32, batch_f32,
                        batch_f32, batch_b16, batch_b16, batch_b16,
                        pltpu.VMEM((nbatch, c, c), _BF16),
                        pltpu.VMEM((nbatch, 1, _LANES), _F32),
                        pltpu.VMEM((gh, _HEAD_DIM, _HEAD_DIM), _F32)],
        compiler_params=_params(("parallel", "parallel", "arbitrary")),
        name="gated_deltanet",
    )(ab_g, proj, proj, proj, proj, conv_w, conv_w, conv_w, alog_g, dtb_g,
      norm_w.reshape(1, _HEAD_DIM).astype(_F32), tri)


def _router_kernel(x_ref, w_ref, b_ref, tri_ref, route_ref, counts_ref, carry_ref, *, tm):
    i = pl.program_id(0)

    @pl.when(i == 0)
    def _():
        carry_ref[...] = jnp.zeros_like(carry_ref)

    logits = _dot_x3(x_ref[...], w_ref[...]) + b_ref[...]
    lane = lax.broadcasted_iota(jnp.int32, (tm, _LANES), 1)
    big = jnp.int32(_LANES)

    def top(vals):
        m = jnp.max(vals, axis=1, keepdims=True)
        idx = jnp.min(jnp.where(vals == m, lane, big), axis=1, keepdims=True)
        return m, idx

    gl = jnp.where(lane < _N_GROUPS, logits, _NEG)
    gm, gidx = top(gl)
    g_w = 1.0 / jnp.sum(jnp.exp(gl - gm), axis=1, keepdims=True)
    lane_group = lax.shift_right_arithmetic(lane - _N_GROUPS, int(math.log2(_EXPERTS_PER_GROUP)))
    el = jnp.where(lane_group == gidx, logits, _NEG)
    m1, i1 = top(el)
    m2, i2 = top(jnp.where(lane == i1, _NEG, el))
    r = jnp.exp(m2 - m1)
    w1 = g_w / (1.0 + r)
    w2 = g_w * r / (1.0 + r)
    e1, e2 = i1 - _N_GROUPS, i2 - _N_GROUPS
    oh1 = (lane == e1).astype(_F32)
    oh2 = (lane == e2).astype(_F32)
    both = oh1 + oh2
    before = _dot(tri_ref[...], both.astype(_BF16)) + carry_ref[0:1, :]
    rank1 = jnp.sum(oh1 * before, axis=1, keepdims=True)
    rank2 = jnp.sum(oh2 * before, axis=1, keepdims=True)
    out = jnp.zeros((tm, _LANES), _F32)
    for pos, val in enumerate((e1.astype(_F32), e2.astype(_F32), w1, w2, rank1, rank2)):
        out = jnp.where(lane == pos, val, out)
    route_ref[...] = out
    total = carry_ref[0:1, :] + jnp.sum(both, axis=0, keepdims=True)
    carry_ref[...] = jnp.broadcast_to(total, carry_ref.shape)
    counts_ref[...] = carry_ref[...]


def _router(x32, w_r, b_r, tm):
    t, d = x32.shape
    tm = min(tm, t)
    ids = lax.broadcasted_iota(jnp.int32, (tm, tm), 0)
    tri = (ids > ids.T).astype(_BF16)
    return pl.pallas_call(
        functools.partial(_router_kernel, tm=tm),
        out_shape=(jax.ShapeDtypeStruct((t, _LANES), _F32), jax.ShapeDtypeStruct((8, _LANES), _F32)),
        grid=(t // tm,),
        in_specs=[pl.BlockSpec((tm, d), lambda i: (i, 0)),
                  pl.BlockSpec((d, _LANES), lambda i: (0, 0)),
                  pl.BlockSpec((1, _LANES), lambda i: (0, 0)),
                  pl.BlockSpec((tm, tm), lambda i: (0, 0))],
        out_specs=(pl.BlockSpec((tm, _LANES), lambda i: (i, 0)),
                   pl.BlockSpec((8, _LANES), lambda i: (0, 0))),
        scratch_shapes=[pltpu.VMEM((8, _LANES), _F32)],
        compiler_params=_params(("arbitrary",)),
        name="moe_router",
    )(x32, w_r, b_r, tri)


_GATHER_UNROLL = 8


def _gather_rows(src_hbm, row_of, dst_ref, sem, n):
    def issue(r, _):
        pltpu.make_async_copy(src_hbm.at[pl.ds(row_of(r), 1)], dst_ref.at[pl.ds(r, 1)], sem).start()
        return 0

    lax.fori_loop(0, n, issue, 0, unroll=_GATHER_UNROLL)


def _wait_rows(src_hbm, dst_ref, sem, n):
    pltpu.make_async_copy(src_hbm.at[pl.ds(0, n)], dst_ref, sem).wait()


def _expert_kernel(te_ref, nact_ref, tok_ref, x_hbm, wg_ref, wu_ref, wd_ref, o_ref, buf_ref, sems, *, tm):
    i = pl.program_id(0)
    nact = nact_ref[0]
    slot = lax.rem(i, 2)

    def start(tile, s):
        base = tile * tm
        _gather_rows(x_hbm, lambda r: tok_ref[base + r], buf_ref.at[s], sems.at[s], tm)

    @pl.when(i == 0)
    def _():
        start(0, 0)

    @pl.when(i + 1 < nact)
    def _():
        start(i + 1, 1 - slot)

    @pl.when(i < nact)
    def _():
        _wait_rows(x_hbm, buf_ref.at[slot], sems.at[slot], tm)
        x = buf_ref[slot].astype(_BF16)
        hg = _dot(x, wg_ref[0])
        hu = _dot(x, wu_ref[0])
        h = (hg * _sigmoid(hg)) * hu
        o_ref[...] = _dot(h.astype(_BF16), wd_ref[0])

    @pl.when(i >= nact)
    def _():
        o_ref[...] = jnp.zeros_like(o_ref)


def _expert_ffn(x32, tok_of_slot, w_gate, w_up, w_down, layer, tile_expert, nact, n_tiles, tm):
    t, d = x32.shape
    f = w_gate.shape[3]
    return pl.pallas_call(
        functools.partial(_expert_kernel, tm=tm),
        out_shape=jax.ShapeDtypeStruct((n_tiles * tm, d), _F32),
        grid_spec=pltpu.PrefetchScalarGridSpec(
            num_scalar_prefetch=3,
            grid=(n_tiles,),
            in_specs=[pl.BlockSpec(memory_space=pl.ANY),
                      pl.BlockSpec((None, 1, d, f), lambda i, te, na, tok: (layer, te[i], 0, 0)),
                      pl.BlockSpec((None, 1, d, f), lambda i, te, na, tok: (layer, te[i], 0, 0)),
                      pl.BlockSpec((None, 1, f, d), lambda i, te, na, tok: (layer, te[i], 0, 0))],
            out_specs=pl.BlockSpec((tm, d), lambda i, te, na, tok: (i, 0)),
            scratch_shapes=[pltpu.VMEM((2, tm, d), _F32), pltpu.SemaphoreType.DMA((2,))]),
        compiler_params=_params(("arbitrary",)),
        name="moe_expert_ffn",
    )(tile_expert, nact, tok_of_slot, x32, w_gate, w_up, w_down)


def _combine_kernel(pos_ref, ys_hbm, route_ref, x_ref, g_ref, b_ref, o32_ref, o16_ref, buf_ref, sems,
                    *, tc, alpha):
    i = pl.program_id(0)
    slot = lax.rem(i, 2)

    def start(step, s):
        base = step * tc
        for k in range(2):
            _gather_rows(ys_hbm, lambda r, k=k: pos_ref[2 * (base + r) + k], buf_ref.at[s, k],
                         sems.at[s, k], tc)

    @pl.when(i == 0)
    def _():
        start(0, 0)

    @pl.when(i + 1 < pl.num_programs(0))
    def _():
        start(i + 1, 1 - slot)

    for k in range(2):
        _wait_rows(ys_hbm, buf_ref.at[slot, k], sems.at[slot, k], tc)
    route = route_ref[...]
    w1, w2 = route[:, 2:3], route[:, 3:4]
    y = alpha * x_ref[...] + (w1 * buf_ref[slot, 0] + w2 * buf_ref[slot, 1])
    out = _layer_norm_rows(y, g_ref[...], b_ref[...])
    o32_ref[...] = out
    o16_ref[...] = out.astype(_BF16)


def _combine_norm(ys, pos_flat, route, x32, g, b, alpha, tc):
    t, d = x32.shape
    tc = min(tc, t)
    row = lambda: pl.BlockSpec((tc, d), lambda i, pos: (i, 0))
    vec = lambda: pl.BlockSpec((1, d), lambda i, pos: (0, 0))
    return pl.pallas_call(
        functools.partial(_combine_kernel, tc=tc, alpha=alpha),
        out_shape=(jax.ShapeDtypeStruct((t, d), _F32), jax.ShapeDtypeStruct((t, d), _BF16)),
        grid_spec=pltpu.PrefetchScalarGridSpec(
            num_scalar_prefetch=1,
            grid=(t // tc,),
            in_specs=[pl.BlockSpec(memory_space=pl.ANY),
                      pl.BlockSpec((tc, _LANES), lambda i, pos: (i, 0)), row(), vec(), vec()],
            out_specs=(row(), row()),
            scratch_shapes=[pltpu.VMEM((2, 2, tc, d), _F32), pltpu.SemaphoreType.DMA((2, 2))]),
        compiler_params=_params(("arbitrary",)),
        name="moe_combine_norm",
    )(pos_flat, ys, route, x32, g.reshape(1, d), b.reshape(1, d))


def _moe_layer(x32, w_rg, b_rg, w_re, b_re, w_gate, w_up, w_down, layer, ln_g, ln_b, alpha, tm_r, tm_e, tc):
    t, d = x32.shape
    pad = _LANES - _N_GROUPS - _N_EXPERTS
    w_r = jnp.pad(jnp.concatenate([w_rg, w_re], axis=1).astype(_F32), ((0, 0), (0, pad)))
    b_r = jnp.pad(jnp.concatenate([b_rg, b_re]).astype(_F32), (0, pad)).reshape(1, _LANES)
    route, counts = _router(x32, w_r, b_r, tm_r)

    counts = counts[0, :_N_EXPERTS].astype(jnp.int32)
    padded = ((counts + tm_e - 1) // tm_e) * tm_e
    ends = jnp.cumsum(padded)
    starts = ends - padded
    eid = route[:, 0:2].astype(jnp.int32)
    pos = starts[eid] + route[:, 4:6].astype(jnp.int32)
    n_tiles = (2 * t) // tm_e + _N_EXPERTS
    nact = (ends[-1] // tm_e).astype(jnp.int32).reshape(1)
    tile_start = jnp.minimum(jnp.arange(n_tiles, dtype=jnp.int32), nact[0] - 1) * tm_e
    tile_expert = jnp.minimum(jnp.sum((ends[None, :] <= tile_start[:, None]).astype(jnp.int32), axis=1),
                              _N_EXPERTS - 1)
    tok = jnp.broadcast_to(jnp.arange(t, dtype=jnp.int32)[:, None], (t, 2))
    tok_of_slot = jnp.zeros((n_tiles * tm_e,), jnp.int32).at[pos.reshape(-1)].set(tok.reshape(-1))

    ys = _expert_ffn(x32, tok_of_slot, w_gate, w_up, w_down, layer, tile_expert, nact, n_tiles, tm_e)
    return _combine_norm(ys, pos.reshape(-1), route, x32, ln_g, ln_b, alpha, tc)


def kernel(x, w_in, conv_w, gdn_a_log, gdn_dt_bias, gdn_norm_w, diff_lambda_q1, diff_lambda_k1,
           diff_lambda_q2, diff_lambda_k2, diff_subln_w, w_branch_sba, w_branch_gdn, w_branch_diff,
           w_out, ln1_g, ln1_b, w_router_group, b_router_group, w_router_expert, b_router_expert,
           w_expert_gate, w_expert_up, w_expert_down, ln2_g, ln2_b):
    bsz, seq, d = x.shape
    depth = w_in.shape[0]
    t = bsz * seq
    heads = d // 256
    w = heads * _HEAD_DIM
    alpha = (2 * depth) ** 0.25
    big = t >= 8192
    tm_e = 256 if big else 64

    tables = _rope_tables(seq)
    ab0 = 3 * w + 4 * w
    ab1 = ab0 + 2 * heads

    w_a = w_in[:, :, :ab0].astype(_BF16)
    w_b = w_in[:, :, ab1:].astype(_BF16)
    w_ab = jnp.pad(w_in[:, :, ab0:ab1], ((0, 0), (0, 0), (0, _LANES - 2 * heads))).astype(_BF16)
    wb_sba, wb_gdn, wb_diff = (wb.astype(_BF16) for wb in (w_branch_sba, w_branch_gdn, w_branch_diff))
    w_out16 = w_out.astype(_BF16)
    w_gate16, w_up16, w_down16 = (we.astype(_BF16) for we in (w_expert_gate, w_expert_up, w_expert_down))

    x32 = x.reshape(t, d)
    x16 = x32.astype(_BF16)
    for l in range(depth):
        proj_a = _matmul(x16, w_a, l, _BF16, 1024, 512, "in_proj_a")
        proj_b = _matmul(x16, w_b, l, _BF16, 1024, 512, "in_proj_b")
        ab = _matmul(x16, w_ab, l, _F32, 1024, _LANES, "in_proj_ab")

        y_sba = _stick_breaking(proj_a, bsz, seq, heads, 512, 256)
        y_gdn = _gated_deltanet(proj_a, ab, conv_w[l], gdn_a_log[l], gdn_dt_bias[l], gdn_norm_w[l],
                                bsz, seq, heads, 512, 4)
        q_r, k_r = _rope(proj_b, tables, bsz, seq, heads, 512)
        lam_vecs = jnp.stack([diff_lambda_q1[l], diff_lambda_k1[l], diff_lambda_q2[l],
                              diff_lambda_k2[l]]).astype(_F32)
        lam_init = 0.8 - 0.6 * math.exp(-0.3 * l)
        y_diff = _diff_attention(q_r, k_r, proj_b, lam_vecs, diff_subln_w[l].astype(_F32), lam_init,
                                 bsz, seq, heads, 512, 256)

        merged = _branch_merge(y_sba, y_gdn, y_diff, wb_sba, wb_gdn, wb_diff, l, proj_b, 3 * w, 512, 512)
        h = _outproj_residual(merged, w_out16, l, x32, alpha, 1024, 512)
        x32, x16 = _layer_norm(h, ln1_g[l], ln1_b[l], 256)

        x32, x16 = _moe_layer(x32, w_router_group[l], b_router_group[l], w_router_expert[l],
                              b_router_expert[l], w_gate16, w_up16, w_down16, l,
                              ln2_g[l], ln2_b[l], alpha, 512, tm_e, 128)
    return x32.reshape(bsz, seq, d)
```

```python
import functools
import math

import jax
import jax.numpy as jnp
from jax import lax
from jax.experimental import pallas as pl
from jax.experimental.pallas import tpu as pltpu

_F32 = jnp.float32
_BF16 = jnp.bfloat16

_LANES = 128
_VMEM_LIMIT = 56 * 1024 * 1024
_HEAD_DIM = 128
_DIFF_DH = 64
_GDN_CHUNK = 64
_GDN_SUB = 16
_CONV_K = 4
_ROPE_THETA = 10000.0
_N_GROUPS = 4
_EXPERTS_PER_GROUP = 8
_N_EXPERTS = _N_GROUPS * _EXPERTS_PER_GROUP
_LN_EPS = 1e-5
_NORM_EPS = 1e-6
_NEG = -1e30
_LOG2E = 1.4426950408889634


def _params(sem):
    return pltpu.CompilerParams(dimension_semantics=sem, vmem_limit_bytes=_VMEM_LIMIT)


def _nt_dot(a, b):
    return lax.dot_general(a, b, (((1,), (1,)), ((), ())), preferred_element_type=_F32)


def _dot(a, b):
    return jnp.dot(a, b, preferred_element_type=_F32)


def _split2(a):
    hi = a.astype(_BF16)
    lo = (a - hi.astype(_F32)).astype(_BF16)
    return hi, lo


def _dot_x3(a, b, nt=False):
    f = _nt_dot if nt else _dot
    ah, al = _split2(a)
    bh, bl = _split2(b)
    return f(ah, bh) + (f(ah, bl) + f(al, bh))


def _dot_exact_lhs(a_bf16, b, nt=False):
    f = _nt_dot if nt else _dot
    b1 = b.astype(_BF16)
    r1 = b - b1.astype(_F32)
    b2 = r1.astype(_BF16)
    b3 = (r1 - b2.astype(_F32)).astype(_BF16)
    return f(a_bf16, b1) + (f(a_bf16, b2) + f(a_bf16, b3))


def _sigmoid(x):
    return 1.0 / (1.0 + jnp.exp(-x))


def _softplus(x):
    return jnp.maximum(x, 0.0) + jnp.log(1.0 + jnp.exp(-jnp.abs(x)))


def _mm_kernel(x_ref, w_ref, o_ref):
    o_ref[...] = _nt_dot(x_ref[...], w_ref[0]).astype(o_ref.dtype)


def _matmul_nt(x, w, layer, row0, n, out_dtype, tm, tn, name):
    m, k = x.shape
    tm, tn = min(tm, m), min(tn, n)
    while n % tn:
        tn //= 2
    return pl.pallas_call(
        _mm_kernel,
        out_shape=jax.ShapeDtypeStruct((m, n), out_dtype),
        grid=(m // tm, n // tn),
        in_specs=[pl.BlockSpec((tm, k), lambda i, j: (i, 0)),
                  pl.BlockSpec((pl.Element(1), pl.Element(tn), pl.Element(k)),
                               lambda i, j: (layer, pl.multiple_of(row0 + j * tn, math.gcd(row0, tn)), 0))],
        out_specs=pl.BlockSpec((tm, tn), lambda i, j: (i, j)),
        compiler_params=_params(("parallel", "arbitrary")),
        name=name,
    )(x, w)


def _merge_kernel(ys_ref, yg_ref, yd_ref, ws_ref, wg_ref, wd_ref, gs_ref, gg_ref, gd_ref, o_ref):
    acc = _sigmoid(gs_ref[...].astype(_F32)) * _dot(ys_ref[...], ws_ref[...])
    acc += _sigmoid(gg_ref[...].astype(_F32)) * _dot(yg_ref[...], wg_ref[...])
    acc += _sigmoid(gd_ref[...].astype(_F32)) * _dot(yd_ref[...], wd_ref[...])
    o_ref[...] = acc.astype(o_ref.dtype)


def _branch_merge(y_sba, y_gdn, y_diff, wb_sba, wb_gdn, wb_diff, layer, proj, gate_col0, tm, tn):
    t, w = y_sba.shape
    d = wb_sba.shape[2]
    tm, tn = min(tm, t), min(tn, d)
    g0 = gate_col0 // tn
    nd = d // tn
    y_spec = pl.BlockSpec((tm, w), lambda i, j: (i, 0))
    w_spec = pl.BlockSpec((None, w, tn), lambda i, j: (layer, 0, j))

    def gate_spec(b):
        return pl.BlockSpec((tm, tn), lambda i, j: (i, g0 + b * nd + j))

    return pl.pallas_call(
        _merge_kernel,
        out_shape=jax.ShapeDtypeStruct((t, d), _BF16),
        grid=(t // tm, nd),
        in_specs=[y_spec, y_spec, y_spec, w_spec, w_spec, w_spec,
                  gate_spec(0), gate_spec(1), gate_spec(2)],
        out_specs=pl.BlockSpec((tm, tn), lambda i, j: (i, j)),
        compiler_params=_params(("parallel", "arbitrary")),
        name="branch_merge",
    )(y_sba, y_gdn, y_diff, wb_sba, wb_gdn, wb_diff, proj, proj, proj)


def _outproj_kernel(m_ref, w_ref, x_ref, o_ref, *, alpha):
    o_ref[...] = alpha * x_ref[...] + _dot(m_ref[...], w_ref[...])


def _outproj_residual(merged, w_out, layer, x, alpha, tm, tn):
    t, d = merged.shape
    tm, tn = min(tm, t), min(tn, d)
    return pl.pallas_call(
        functools.partial(_outproj_kernel, alpha=alpha),
        out_shape=jax.ShapeDtypeStruct((t, d), _F32),
        grid=(t // tm, d // tn),
        in_specs=[pl.BlockSpec((tm, d), lambda i, j: (i, 0)),
                  pl.BlockSpec((None, d, tn), lambda i, j: (layer, 0, j)),
                  pl.BlockSpec((tm, tn), lambda i, j: (i, j))],
        out_specs=pl.BlockSpec((tm, tn), lambda i, j: (i, j)),
        compiler_params=_params(("parallel", "arbitrary")),
        name="outproj_residual",
    )(merged, w_out, x)


def _layer_norm_rows(y, g, b):
    mu = jnp.mean(y, axis=-1, keepdims=True)
    yc = y - mu
    var = jnp.mean(yc * yc, axis=-1, keepdims=True)
    return yc * lax.rsqrt(var + _LN_EPS) * g + b


def _ln_kernel(y_ref, g_ref, b_ref, o32_ref, o16_ref):
    out = _layer_norm_rows(y_ref[...], g_ref[...], b_ref[...])
    o32_ref[...] = out
    o16_ref[...] = out.astype(_BF16)


def _layer_norm(y, g, b, tr):
    t, d = y.shape
    tr = min(tr, t)
    row = pl.BlockSpec((tr, d), lambda i: (i, 0))
    vec = pl.BlockSpec((1, d), lambda i: (0, 0))
    return pl.pallas_call(
        _ln_kernel,
        out_shape=(jax.ShapeDtypeStruct((t, d), _F32), jax.ShapeDtypeStruct((t, d), _BF16)),
        grid=(t // tr,),
        in_specs=[row, vec, vec],
        out_specs=(row, row),
        compiler_params=_params(("parallel",)),
        name="layer_norm",
    )(y, g.reshape(1, d), b.reshape(1, d))


def _neg_abs(x):
    bits = lax.bitcast_convert_type(x, jnp.uint32) | jnp.uint32(0x80000000)
    return lax.bitcast_convert_type(bits, _F32)


_SWEEP_UNROLLS = (8, 4, 2)


def _causal_sweep(run, n_blocks, carry):
    done = 0
    for un in _SWEEP_UNROLLS:
        trips = (n_blocks - done) // un
        first = n_blocks - 1 - done
        carry = lax.fori_loop(0, trips, lambda i, c, un=un, first=first: run(first - un * i, c, un, False),
                              carry)
        done = done + trips * un
    return carry


def _sba_kernel(q_ref, k_ref, v_ref, u_ref, o_ref, acc_ref, w_ref, *, tq, tk, scale):
    qi = pl.program_id(2)
    q = (q_ref[...].astype(_F32) * (scale * _LOG2E)).astype(_BF16)
    u = u_ref[...]
    n_diag = tq // tk
    acc_ref[...] = jnp.zeros_like(acc_ref)

    def scores(kb):
        ks = pl.multiple_of(jnp.maximum(kb, 0) * tk, tk)
        return _nt_dot(q, k_ref[pl.ds(ks, tk), :])

    def consume(w, kb, carry, masked):
        v = v_ref[pl.ds(pl.multiple_of(kb * tk, tk), tk), :]
        sp = jnp.maximum(w, 0.0) + jnp.log(1.0 + jnp.exp2(_neg_abs(w))) * _LOG2E
        if masked:
            rows = qi * tq + lax.broadcasted_iota(jnp.int32, (tq, tk), 0)
            cols = kb * tk + lax.broadcasted_iota(jnp.int32, (tq, tk), 1)
            strict = cols < rows
            sp = jnp.where(strict, sp, 0.0)
        later = _dot(sp.astype(_BF16), u)
        att = jnp.exp2(((w - sp) - later) - carry)
        if masked:
            att = jnp.where(strict, att, 0.0)
        acc_ref[...] += _dot(att.astype(_BF16), v)
        return carry + jnp.sum(sp, axis=1, keepdims=True)

    def run(kb0, carry, nblk, masked):
        for j in range(nblk):
            w_ref[(j + 1) % 2] = scores(kb0 - j - 1)
            carry = consume(w_ref[j % 2], kb0 - j, carry, masked)
        return carry

    top = (qi + 1) * n_diag - 1
    w_ref[0] = scores(top)
    carry = run(top, jnp.zeros((tq, 1), _F32), n_diag, True)
    n_full = qi * n_diag
    _causal_sweep(run, n_full, carry)
    o_ref[...] = acc_ref[...].astype(o_ref.dtype)


def _stick_breaking(proj, bsz, seq, heads, tq, tk):
    t = bsz * seq
    tq, tk = min(tq, seq), min(tk, seq)
    tk = min(tk, tq)
    nq = seq // tq
    assert (tq // tk) % 2 == 0, "the two score slots alternate per key block"
    ids = lax.broadcasted_iota(jnp.int32, (tk, tk), 0)
    u = (ids > ids.T).astype(_BF16)
    return pl.pallas_call(
        functools.partial(_sba_kernel, tq=tq, tk=tk, scale=_HEAD_DIM ** -0.5),
        out_shape=jax.ShapeDtypeStruct((t, heads * _HEAD_DIM), _BF16),
        grid=(bsz, heads, nq),
        in_specs=[pl.BlockSpec((tq, _HEAD_DIM), lambda b, h, i: (b * nq + i, h)),
                  pl.BlockSpec((seq, _HEAD_DIM), lambda b, h, i: (b, heads + h)),
                  pl.BlockSpec((seq, _HEAD_DIM), lambda b, h, i: (b, 2 * heads + h)),
                  pl.BlockSpec((tk, tk), lambda b, h, i: (0, 0))],
        out_specs=pl.BlockSpec((tq, _HEAD_DIM), lambda b, h, i: (b * nq + i, h)),
        scratch_shapes=[pltpu.VMEM((tq, _HEAD_DIM), _F32), pltpu.VMEM((2, tq, tk), _F32)],
        compiler_params=_params(("parallel", "parallel", "arbitrary")),
        name="stick_breaking_attention",
    )(proj, proj, proj, u)


def _rope_kernel(q_ref, k_ref, cos_ref, sa_ref, sb_ref, qo_ref, ko_ref, *, heads, scale):
    cos, sa, sb = cos_ref[...], sa_ref[...], sb_ref[...]
    for h in range(heads):
        sl = slice(h * _HEAD_DIM, (h + 1) * _HEAD_DIM)
        for src, dst, s in ((q_ref, qo_ref, scale), (k_ref, ko_ref, 1.0)):
            x = src[:, sl].astype(_F32)
            r = (x * cos + pltpu.roll(x, _HEAD_DIM - _DIFF_DH // 2, axis=1) * sa
                 + pltpu.roll(x, _DIFF_DH // 2, axis=1) * sb)
            dst[:, sl] = (r * s).astype(dst.dtype)


def _rope_tables(seq):
    half = _DIFF_DH // 2
    pos = jnp.arange(seq, dtype=_F32)
    inv_freq = _ROPE_THETA ** (-jnp.arange(0, _DIFF_DH, 2, dtype=_F32) / _DIFF_DH)
    ang = pos[:, None] * inv_freq[None, :]
    cos, sin, zero = jnp.cos(ang), jnp.sin(ang), jnp.zeros_like(ang)
    cos_t = jnp.concatenate([cos] * 4, axis=-1)
    sa_t = jnp.concatenate([-sin, zero] * 2, axis=-1)
    sb_t = jnp.concatenate([zero, sin] * 2, axis=-1)
    return cos_t, sa_t, sb_t


def _rope(proj, tables, bsz, seq, heads, ts):
    t = bsz * seq
    w = heads * _HEAD_DIM
    ts = min(ts, seq)
    ns = seq // ts
    tab = pl.BlockSpec((ts, _HEAD_DIM), lambda i: (i % ns, 0))
    out = pl.BlockSpec((ts, w), lambda i: (i, 0))
    return pl.pallas_call(
        functools.partial(_rope_kernel, heads=heads, scale=_DIFF_DH ** -0.5 * _LOG2E),
        out_shape=(jax.ShapeDtypeStruct((t, w), _BF16), jax.ShapeDtypeStruct((t, w), _BF16)),
        grid=(t // ts,),
        in_specs=[pl.BlockSpec((ts, w), lambda i: (i, 0)),
                  pl.BlockSpec((ts, w), lambda i: (i, 1)), tab, tab, tab],
        out_specs=(out, out),
        compiler_params=_params(("parallel",)),
        name="diff_rope",
    )(proj, proj, *tables)


def _diff_kernel(lam_ref, sw_ref, q_ref, k_ref, v_ref, o_ref, acc_ref, l_ref, s_ref, *, tq, tk, lam_init):
    qi = pl.program_id(2)
    q = q_ref[...]
    lane = lax.broadcasted_iota(jnp.int32, (tq, _HEAD_DIM), 1)
    zero = jnp.zeros_like(q)
    qs = (jnp.where(lane < _DIFF_DH, q, zero), jnp.where(lane >= _DIFF_DH, q, zero))
    n_diag = tq // tk
    acc_ref[...] = jnp.zeros_like(acc_ref)
    l_ref[...] = jnp.zeros_like(l_ref)

    def put_scores(slot, kb):
        ks = pl.multiple_of(jnp.maximum(kb, 0) * tk, tk)
        k = k_ref[pl.ds(ks, tk), :]
        for c in range(2):
            s_ref[slot, c] = _nt_dot(qs[c], k)

    def consume(slot, kb, carry, masked):
        v = v_ref[pl.ds(pl.multiple_of(kb * tk, tk), tk), :]
        if masked:
            rows = qi * tq + lax.broadcasted_iota(jnp.int32, (tq, tk), 0)
            cols = kb * tk + lax.broadcasted_iota(jnp.int32, (tq, tk), 1)
            causal = cols <= rows
        out = []
        for c in range(2):
            m_prev = carry[c]
            s = s_ref[slot, c]
            if masked:
                s = jnp.where(causal, s, _NEG)
            m_new = jnp.maximum(m_prev, jnp.max(s, axis=1, keepdims=True))
            p = jnp.exp2(s - m_new)
            alpha = jnp.exp2(m_prev - m_new)
            acc_ref[c] = alpha * acc_ref[c] + _dot(p.astype(_BF16), v)
            lanes = p[:, 0:_LANES]
            for j in range(1, tk // _LANES):
                lanes = lanes + p[:, j * _LANES:(j + 1) * _LANES]
            l_ref[c] = alpha * l_ref[c] + lanes
            out.append(m_new)
        return tuple(out)

    def run(kb0, carry, nblk, masked):
        for j in range(nblk):
            put_scores((j + 1) % 2, kb0 - j - 1)
            carry = consume(j % 2, kb0 - j, carry, masked)
        return carry

    neg = jnp.full((tq, 1), _NEG, _F32)
    top = (qi + 1) * n_diag - 1
    put_scores(0, top)
    carry = run(top, (neg, neg), n_diag, True)
    n_full = qi * n_diag
    _causal_sweep(run, n_full, carry)

    lv = lam_ref[...]
    lam = (jnp.exp(jnp.sum(lv[0:1] * lv[1:2], axis=1, keepdims=True))
           - jnp.exp(jnp.sum(lv[2:3] * lv[3:4], axis=1, keepdims=True)) + lam_init)
    l0 = jnp.sum(l_ref[0], axis=1, keepdims=True)
    l1 = jnp.sum(l_ref[1], axis=1, keepdims=True)
    o = acc_ref[0] / l0 - lam * (acc_ref[1] / l1)
    o = o * lax.rsqrt(jnp.mean(o * o, axis=1, keepdims=True) + _LN_EPS) * sw_ref[...]
    o_ref[...] = (o * (1.0 - lam_init)).astype(o_ref.dtype)


def _diff_attention(q_r, k_r, proj, lam_vecs, subln_w, lam_init, bsz, seq, heads, tq, tk):
    t = bsz * seq
    tq, tk = min(tq, seq), min(tk, seq)
    tk = min(tk, tq)
    nq = seq // tq
    return pl.pallas_call(
        functools.partial(_diff_kernel, tq=tq, tk=tk, lam_init=lam_init),
        out_shape=jax.ShapeDtypeStruct((t, heads * _HEAD_DIM), _BF16),
        grid=(bsz, heads, nq),
        in_specs=[pl.BlockSpec((4, _DIFF_DH), lambda b, h, i: (0, 0)),
                  pl.BlockSpec((1, _HEAD_DIM), lambda b, h, i: (0, 0)),
                  pl.BlockSpec((tq, _HEAD_DIM), lambda b, h, i: (b * nq + i, h)),
                  pl.BlockSpec((seq, _HEAD_DIM), lambda b, h, i: (b, h)),
                  pl.BlockSpec((seq, _HEAD_DIM), lambda b, h, i: (b, 2 * heads + h))],
        out_specs=pl.BlockSpec((tq, _HEAD_DIM), lambda b, h, i: (b * nq + i, h)),
        scratch_shapes=[pltpu.VMEM((2, tq, _HEAD_DIM), _F32), pltpu.VMEM((2, tq, _LANES), _F32),
                        pltpu.VMEM((2, 2, tq, tk), _F32)],
        compiler_params=_params(("parallel", "parallel", "arbitrary")),
        name="differential_attention",
    )(lam_vecs, subln_w.reshape(1, _HEAD_DIM), q_r, k_r, proj)


def _bdot(a, b):
    return lax.dot_general(a, b, (((2,), (1,)), ((0,), (0,))), preferred_element_type=_F32)


def _bdot_nt(a, b):
    return lax.dot_general(a, b, (((2,), (2,)), ((0,), (0,))), preferred_element_type=_F32)


def _bdot_tn(a, b):
    return lax.dot_general(a, b, (((1,), (1,)), ((0,), (0,))), preferred_element_type=_F32)


def _bdot_x3(a, b):
    ah, al = _split2(a)
    bh, bl = _split2(b)
    return _bdot(ah, bh) + (_bdot(ah, bl) + _bdot(al, bh))


def _unit_lower_inverse(lm, eye, blockdiag):
    c = lm.shape[-1]
    md = jnp.where(blockdiag, -lm, 0.0)
    x = eye + md
    p = _bdot_x3(md, md)
    steps = int(math.log2(_GDN_SUB)) - 1
    for it in range(steps):
        x = x + _bdot_x3(x, p)
        if it + 1 < steps:
            p = _bdot_x3(p, p)
    n = _bdot_x3(x, jnp.where(blockdiag, 0.0, lm))
    y = eye - n
    pw = n
    for _ in range(int(math.log2(c // _GDN_SUB)) - 1):
        pw = _bdot_x3(pw, pw)
        y = y + _bdot_x3(y, pw)
    return _bdot_x3(y, x)


def _gdn_kernel(ab_ref, q_ref, k_ref, v_ref, z_ref, wq_ref, wk_ref, wv_ref, alog_ref, dtb_ref, nw_ref,
                tri_ref, o_ref, xp_ref, qs_ref, ks_ref, vs_ref, dm_ref, gc_ref, bt_ref,
                u_ref, w_ref, qd_ref, kd_ref, in_ref, dl_ref, state_ref, *, tb, gh):
    i = pl.program_id(2)
    c = _GDN_CHUNK
    nc = tb // c
    halo = 8

    @pl.when(i == 0)
    def _():
        xp_ref[:, 0:halo, :] = jnp.zeros((3, halo, gh * _HEAD_DIM), _F32)
        state_ref[...] = jnp.zeros_like(state_ref)

    @pl.when(i > 0)
    def _():
        xp_ref[:, 0:halo, :] = xp_ref[:, tb:tb + halo, :]

    for idx, (src, cw_ref, dst) in enumerate(((q_ref, wq_ref, qs_ref), (k_ref, wk_ref, ks_ref),
                                              (v_ref, wv_ref, vs_ref))):
        xp_ref[idx, halo:halo + tb, :] = src[...].astype(_F32)
        w = cw_ref[...]
        y = xp_ref[idx, halo - 3:halo - 3 + tb, :] * w[0:1, :]
        for j in range(1, _CONV_K):
            y = y + xp_ref[idx, halo - 3 + j:halo - 3 + j + tb, :] * w[j:j + 1, :]
        y = y * _sigmoid(y)
        for g in range(gh):
            yh = y[:, g * _HEAD_DIM:(g + 1) * _HEAD_DIM]
            if idx < 2:
                yh = yh * lax.rsqrt(jnp.sum(yh * yh, axis=1, keepdims=True) + _NORM_EPS)
            if idx == 0:
                yh = yh * (_HEAD_DIM ** -0.5)
            for ci in range(nc):
                dst[ci * gh + g] = yh[ci * c:(ci + 1) * c, :]

    ab = ab_ref[0]
    g_all = -jnp.exp(alog_ref[0]) * _softplus(ab + dtb_ref[0])
    beta_all = _sigmoid(ab)
    tri = tri_ref[...]
    pick = (lax.broadcasted_iota(jnp.int32, (8, _LANES), 0)
            == lax.broadcasted_iota(jnp.int32, (8, _LANES), 1)).astype(_BF16)
    for ci in range(nc):
        rows = slice(ci * c, (ci + 1) * c)
        gcum = _dot_exact_lhs(tri, g_all[rows, :])
        gcum_t = _dot_exact_lhs(pick, gcum, nt=True)
        for g in range(gh):
            gcol = gcum[:, g:g + 1]
            dm_ref[ci * gh + g] = gcol - gcum_t[g:g + 1, :]
            gc_ref[ci * gh + g] = jnp.broadcast_to(gcol, (c, _LANES))
            bt_ref[ci * gh + g] = jnp.broadcast_to(beta_all[rows, gh + g:gh + g + 1], (c, _LANES))

    ri = lax.broadcasted_iota(jnp.int32, (c, c), 0)
    cj = lax.broadcasted_iota(jnp.int32, (c, c), 1)
    incl, strict = ri >= cj, ri > cj
    sub_shift = int(math.log2(_GDN_SUB))
    blockdiag = lax.shift_right_logical(ri, sub_shift) == lax.shift_right_logical(cj, sub_shift)
    eye = (ri == cj).astype(_F32)

    gc = gc_ref[...]
    beta = bt_ref[...]
    decay = jnp.where(incl, jnp.exp(jnp.where(incl, dm_ref[...], 0.0)), 0.0)
    g_last = gc[:, c - 1:c, :]
    eg = jnp.exp(gc)
    dl_ref[...] = jnp.exp(g_last)
    q, k, v = qs_ref[...], ks_ref[...], vs_ref[...]
    k16 = k.astype(_BF16)
    kb = k * beta
    lm = jnp.where(strict, _bdot_nt(kb.astype(_BF16), k16) * decay, 0.0)
    tinv = _unit_lower_inverse(lm, eye, blockdiag)
    u_ref[...] = _bdot_x3(tinv, v * beta)
    w_ref[...] = _bdot_x3(tinv, kb * eg).astype(_BF16)
    in_ref[...] = (_bdot_nt(q.astype(_BF16), k16) * decay).astype(_BF16)
    qd_ref[...] = (q * eg).astype(_BF16)
    kd_ref[...] = (k * jnp.exp(g_last - gc)).astype(_BF16)

    nw = nw_ref[...]

    def chunk(cidx, _):
        sl = pl.ds(cidx * gh, gh)
        st = state_ref[...]
        st16 = st.astype(_BF16)
        v_new = u_ref[sl] - _bdot(w_ref[sl], st16)
        vn16 = v_new.astype(_BF16)
        o = _bdot(qd_ref[sl], st16) + _bdot(in_ref[sl], vn16)
        state_ref[...] = st * dl_ref[sl] + _bdot_tn(kd_ref[sl], vn16)
        o = o * lax.rsqrt(jnp.mean(o * o, axis=2, keepdims=True) + _NORM_EPS) * nw
        rows = pl.ds(pl.multiple_of(cidx * c, c), c)
        for g in range(gh):
            lanes = slice(g * _HEAD_DIM, (g + 1) * _HEAD_DIM)
            zf = z_ref[rows, lanes].astype(_F32)
            o_ref[rows, lanes] = (o[g] * (zf * _sigmoid(zf))).astype(o_ref.dtype)
        return 0

    lax.fori_loop(0, nc, chunk, 0)


def _gated_deltanet(proj, ab, conv_w, a_log, dt_bias, norm_w, bsz, seq, heads, tb, gh):
    t = bsz * seq
    w = heads * _HEAD_DIM
    tb = min(tb, seq)
    gh = min(gh, heads)
    assert gh <= 8, "one sublane tile holds the transposed per-head cumulative gates"
    nb = seq // tb
    ng = heads // gh
    c = _GDN_CHUNK
    nbatch = (tb // c) * gh
    ids = lax.broadcasted_iota(jnp.int32, (c, c), 0)
    tri = (ids >= ids.T).astype(_BF16)

    def group(x):
        return jnp.moveaxis(x.reshape(x.shape[0], ng, gh), 1, 0)

    lane_pad = ((0, 0), (0, 0), (0, _LANES - 2 * gh))
    ab_g = jnp.pad(jnp.concatenate([group(ab[:, :heads]), group(ab[:, heads:2 * heads])], axis=2), lane_pad)
    zeros = jnp.zeros((1, heads), _F32)
    alog_g = jnp.pad(jnp.concatenate([group(a_log.astype(_F32)[None]), group(zeros)], axis=2), lane_pad)
    dtb_g = jnp.pad(jnp.concatenate([group(dt_bias.astype(_F32)[None]), group(zeros)], axis=2), lane_pad)
    blk = (tb, gh * _HEAD_DIM)

    def col(base):
        return pl.BlockSpec(blk, lambda b, g, i: (b * nb + i, base * ng + g))

    def wcol(base):
        return pl.BlockSpec((_CONV_K, gh * _HEAD_DIM), lambda b, g, i: (0, base * ng + g))

    vec = pl.BlockSpec((1, 1, _LANES), lambda b, g, i: (g, 0, 0))
    batch_f32 = pltpu.VMEM((nbatch, c, _HEAD_DIM), _F32)
    batch_b16 = pltpu.VMEM((nbatch, c, _HEAD_DIM), _BF16)
    return pl.pallas_call(
        functools.partial(_gdn_kernel, tb=tb, gh=gh),
        out_shape=jax.ShapeDtypeStruct((t, w), _BF16),
        grid=(bsz, ng, nb),
        in_specs=[pl.BlockSpec((1, tb, _LANES), lambda b, g, i: (g, b * nb + i, 0)),
                  col(3), col(4), col(5), col(6), wcol(0), wcol(1), wcol(2), vec, vec,
                  pl.BlockSpec((1, _HEAD_DIM), lambda b, g, i: (0, 0)),
                  pl.BlockSpec((c, c), lambda b, g, i: (0, 0))],
        out_specs=pl.BlockSpec(blk, lambda b, g, i: (b * nb + i, g)),
        scratch_shapes=[pltpu.VMEM((3, tb + 8, gh * _HEAD_DIM), _F32),
                        batch_f32, batch_f32, batch_f32,
                        pltpu.VMEM((nbatch, c, c), _F32), batch_f32, batch_f32,
                        batch_f32, batch_b16, batch_b16, batch_b16,
                        pltpu.VMEM((nbatch, c, c), _BF16),
                        pltpu.VMEM((nbatch, 1, _LANES), _F32),
                        pltpu.VMEM((gh, _HEAD_DIM, _HEAD_DIM), _F32)],
        compiler_params=_params(("parallel", "parallel", "arbitrary")),
        name="gated_deltanet",
    )(ab_g, proj, proj, proj, proj, conv_w, conv_w, conv_w, alog_g, dtb_g,
      norm_w.reshape(1, _HEAD_DIM).astype(_F32), tri)


def _router_kernel(x_ref, w_ref, b_ref, tri_ref, route_ref, counts_ref, carry_ref, *, tm):
    i = pl.program_id(0)

    @pl.when(i == 0)
    def _():
        carry_ref[...] = jnp.zeros_like(carry_ref)

    logits = _dot_x3(x_ref[...], w_ref[...]) + b_ref[...]
    lane = lax.broadcasted_iota(jnp.int32, (tm, _LANES), 1)
    big = jnp.int32(_LANES)

    def top(vals):
        m = jnp.max(vals, axis=1, keepdims=True)
        idx = jnp.min(jnp.where(vals == m, lane, big), axis=1, keepdims=True)
        return m, idx

    gl = jnp.where(lane < _N_GROUPS, logits, _NEG)
    gm, gidx = top(gl)
    g_w = 1.0 / jnp.sum(jnp.exp(gl - gm), axis=1, keepdims=True)
    lane_group = lax.shift_right_arithmetic(lane - _N_GROUPS, int(math.log2(_EXPERTS_PER_GROUP)))
    el = jnp.where(lane_group == gidx, logits, _NEG)
    m1, i1 = top(el)
    m2, i2 = top(jnp.where(lane == i1, _NEG, el))
    r = jnp.exp(m2 - m1)
    w1 = g_w / (1.0 + r)
    w2 = g_w * r / (1.0 + r)
    e1, e2 = i1 - _N_GROUPS, i2 - _N_GROUPS
    oh1 = (lane == e1).astype(_F32)
    oh2 = (lane == e2).astype(_F32)
    both = oh1 + oh2
    before = _dot(tri_ref[...], both.astype(_BF16)) + carry_ref[0:1, :]
    rank1 = jnp.sum(oh1 * before, axis=1, keepdims=True)
    rank2 = jnp.sum(oh2 * before, axis=1, keepdims=True)
    out = jnp.zeros((tm, _LANES), _F32)
    for pos, val in enumerate((e1.astype(_F32), e2.astype(_F32), w1, w2, rank1, rank2)):
        out = jnp.where(lane == pos, val, out)
    route_ref[...] = out
    total = carry_ref[0:1, :] + jnp.sum(both, axis=0, keepdims=True)
    carry_ref[...] = jnp.broadcast_to(total, carry_ref.shape)
    counts_ref[...] = carry_ref[...]


def _router(x32, w_r, b_r, tm):
    t, d = x32.shape
    tm = min(tm, t)
    ids = lax.broadcasted_iota(jnp.int32, (tm, tm), 0)
    tri = (ids > ids.T).astype(_BF16)
    return pl.pallas_call(
        functools.partial(_router_kernel, tm=tm),
        out_shape=(jax.ShapeDtypeStruct((t, _LANES), _F32), jax.ShapeDtypeStruct((8, _LANES), _F32)),
        grid=(t // tm,),
        in_specs=[pl.BlockSpec((tm, d), lambda i: (i, 0)),
                  pl.BlockSpec((d, _LANES), lambda i: (0, 0)),
                  pl.BlockSpec((1, _LANES), lambda i: (0, 0)),
                  pl.BlockSpec((tm, tm), lambda i: (0, 0))],
        out_specs=(pl.BlockSpec((tm, _LANES), lambda i: (i, 0)),
                   pl.BlockSpec((8, _LANES), lambda i: (0, 0))),
        scratch_shapes=[pltpu.VMEM((8, _LANES), _F32)],
        compiler_params=_params(("arbitrary",)),
        name="moe_router",
    )(x32, w_r, b_r, tri)


_GATHER_UNROLL = 8


def _gather_rows(src_hbm, row_of, dst_ref, sem, n):
    def issue(r, _):
        pltpu.make_async_copy(src_hbm.at[pl.ds(row_of(r), 1)], dst_ref.at[pl.ds(r, 1)], sem).start()
        return 0

    lax.fori_loop(0, n, issue, 0, unroll=_GATHER_UNROLL)


def _wait_rows(src_hbm, dst_ref, sem, n):
    pltpu.make_async_copy(src_hbm.at[pl.ds(0, n)], dst_ref, sem).wait()


def _expert_kernel(te_ref, nact_ref, tok_ref, x_hbm, wg_ref, wu_ref, wd_ref, o_ref, buf_ref, sems, *, tm):
    i = pl.program_id(0)
    nact = nact_ref[0]
    slot = lax.rem(i, 2)

    def start(tile, s):
        base = tile * tm
        _gather_rows(x_hbm, lambda r: tok_ref[base + r], buf_ref.at[s], sems.at[s], tm)

    @pl.when(i == 0)
    def _():
        start(0, 0)

    @pl.when(i + 1 < nact)
    def _():
        start(i + 1, 1 - slot)

    @pl.when(i < nact)
    def _():
        _wait_rows(x_hbm, buf_ref.at[slot], sems.at[slot], tm)
        x = buf_ref[slot].astype(_BF16)
        hg = _dot(x, wg_ref[0])
        hu = _dot(x, wu_ref[0])
        h = (hg * _sigmoid(hg)) * hu
        o_ref[...] = _dot(h.astype(_BF16), wd_ref[0])

    @pl.when(i >= nact)
    def _():
        o_ref[...] = jnp.zeros_like(o_ref)


def _expert_ffn(x32, tok_of_slot, w_gate, w_up, w_down, layer, tile_expert, nact, n_tiles, tm):
    t, d = x32.shape
    f = w_gate.shape[3]
    return pl.pallas_call(
        functools.partial(_expert_kernel, tm=tm),
        out_shape=jax.ShapeDtypeStruct((n_tiles * tm, d), _F32),
        grid_spec=pltpu.PrefetchScalarGridSpec(
            num_scalar_prefetch=3,
            grid=(n_tiles,),
            in_specs=[pl.BlockSpec(memory_space=pl.ANY),
                      pl.BlockSpec((None, 1, d, f), lambda i, te, na, tok: (layer, te[i], 0, 0)),
                      pl.BlockSpec((None, 1, d, f), lambda i, te, na, tok: (layer, te[i], 0, 0)),
                      pl.BlockSpec((None, 1, f, d), lambda i, te, na, tok: (layer, te[i], 0, 0))],
            out_specs=pl.BlockSpec((tm, d), lambda i, te, na, tok: (i, 0)),
            scratch_shapes=[pltpu.VMEM((2, tm, d), _F32), pltpu.SemaphoreType.DMA((2,))]),
        compiler_params=_params(("arbitrary",)),
        name="moe_expert_ffn",
    )(tile_expert, nact, tok_of_slot, x32, w_gate, w_up, w_down)


def _combine_kernel(pos_ref, ys_hbm, route_ref, x_ref, g_ref, b_ref, o32_ref, o16_ref, buf_ref, sems,
                    *, tc, alpha):
    i = pl.program_id(0)
    slot = lax.rem(i, 2)

    def start(step, s):
        base = step * tc
        for k in range(2):
            _gather_rows(ys_hbm, lambda r, k=k: pos_ref[2 * (base + r) + k], buf_ref.at[s, k],
                         sems.at[s, k], tc)

    @pl.when(i == 0)
    def _():
        start(0, 0)

    @pl.when(i + 1 < pl.num_programs(0))
    def _():
        start(i + 1, 1 - slot)

    for k in range(2):
        _wait_rows(ys_hbm, buf_ref.at[slot, k], sems.at[slot, k], tc)
    route = route_ref[...]
    w1, w2 = route[:, 2:3], route[:, 3:4]
    y = alpha * x_ref[...] + (w1 * buf_ref[slot, 0] + w2 * buf_ref[slot, 1])
    out = _layer_norm_rows(y, g_ref[...], b_ref[...])
    o32_ref[...] = out
    o16_ref[...] = out.astype(_BF16)


def _combine_norm(ys, pos_flat, route, x32, g, b, alpha, tc):
    t, d = x32.shape
    tc = min(tc, t)
    row = lambda: pl.BlockSpec((tc, d), lambda i, pos: (i, 0))
    vec = lambda: pl.BlockSpec((1, d), lambda i, pos: (0, 0))
    return pl.pallas_call(
        functools.partial(_combine_kernel, tc=tc, alpha=alpha),
        out_shape=(jax.ShapeDtypeStruct((t, d), _F32), jax.ShapeDtypeStruct((t, d), _BF16)),
        grid_spec=pltpu.PrefetchScalarGridSpec(
            num_scalar_prefetch=1,
            grid=(t // tc,),
            in_specs=[pl.BlockSpec(memory_space=pl.ANY),
                      pl.BlockSpec((tc, _LANES), lambda i, pos: (i, 0)), row(), vec(), vec()],
            out_specs=(row(), row()),
            scratch_shapes=[pltpu.VMEM((2, 2, tc, d), _F32), pltpu.SemaphoreType.DMA((2, 2))]),
        compiler_params=_params(("arbitrary",)),
        name="moe_combine_norm",
    )(pos_flat, ys, route, x32, g.reshape(1, d), b.reshape(1, d))


def _moe_layer(x32, w_rg, b_rg, w_re, b_re, w_gate, w_up, w_down, layer, ln_g, ln_b, alpha, tm_r, tm_e, tc):
    t, d = x32.shape
    pad = _LANES - _N_GROUPS - _N_EXPERTS
    w_r = jnp.pad(jnp.concatenate([w_rg, w_re], axis=1).astype(_F32), ((0, 0), (0, pad)))
    b_r = jnp.pad(jnp.concatenate([b_rg, b_re]).astype(_F32), (0, pad)).reshape(1, _LANES)
    route, counts = _router(x32, w_r, b_r, tm_r)

    counts = counts[0, :_N_EXPERTS].astype(jnp.int32)
    padded = ((counts + tm_e - 1) // tm_e) * tm_e
    ends = jnp.cumsum(padded)
    starts = ends - padded
    eid = route[:, 0:2].astype(jnp.int32)
    pos = starts[eid] + route[:, 4:6].astype(jnp.int32)
    n_tiles = (2 * t) // tm_e + _N_EXPERTS
    nact = (ends[-1] // tm_e).astype(jnp.int32).reshape(1)
    tile_start = jnp.minimum(jnp.arange(n_tiles, dtype=jnp.int32), nact[0] - 1) * tm_e
    tile_expert = jnp.minimum(jnp.sum((ends[None, :] <= tile_start[:, None]).astype(jnp.int32), axis=1),
                              _N_EXPERTS - 1)
    tok = jnp.broadcast_to(jnp.arange(t, dtype=jnp.int32)[:, None], (t, 2))
    tok_of_slot = jnp.zeros((n_tiles * tm_e,), jnp.int32).at[pos.reshape(-1)].set(tok.reshape(-1))

    ys = _expert_ffn(x32, tok_of_slot, w_gate, w_up, w_down, layer, tile_expert, nact, n_tiles, tm_e)
    return _combine_norm(ys, pos.reshape(-1), route, x32, ln_g, ln_b, alpha, tc)


def kernel(x, w_in, conv_w, gdn_a_log, gdn_dt_bias, gdn_norm_w, diff_lambda_q1, diff_lambda_k1,
           diff_lambda_q2, diff_lambda_k2, diff_subln_w, w_branch_sba, w_branch_gdn, w_branch_diff,
           w_out, ln1_g, ln1_b, w_router_group, b_router_group, w_router_expert, b_router_expert,
           w_expert_gate, w_expert_up, w_expert_down, ln2_g, ln2_b):
    bsz, seq, d = x.shape
    depth = w_in.shape[0]
    t = bsz * seq
    heads = d // 256
    w = heads * _HEAD_DIM
    alpha = (2 * depth) ** 0.25
    big = t >= 8192
    tm_e = 256 if big else 64

    tables = _rope_tables(seq)
    ab0 = 3 * w + 4 * w
    ab1 = ab0 + 2 * heads

    w_t = jnp.swapaxes(w_in, 1, 2).astype(_BF16)
    wb_sba, wb_gdn, wb_diff = (wb.astype(_BF16) for wb in (w_branch_sba, w_branch_gdn, w_branch_diff))
    w_out16 = w_out.astype(_BF16)
    w_gate16, w_up16, w_down16 = (we.astype(_BF16) for we in (w_expert_gate, w_expert_up, w_expert_down))

    x32 = x.reshape(t, d)
    x16 = x32.astype(_BF16)
    for l in range(depth):
        proj_a = _matmul_nt(x16, w_t, l, 0, ab0, _BF16, 1024, 1024, "in_proj_a")
        proj_b = _matmul_nt(x16, w_t, l, ab1, w_t.shape[1] - ab1, _BF16, 1024, 1024, "in_proj_b")
        ab = _matmul_nt(x16, w_t, l, ab0, ab1 - ab0, _F32, 1024, _LANES, "in_proj_ab")

        y_sba = _stick_breaking(proj_a, bsz, seq, heads, 512, 256)
        y_gdn = _gated_deltanet(proj_a, ab, conv_w[l], gdn_a_log[l], gdn_dt_bias[l], gdn_norm_w[l],
                                bsz, seq, heads, 256, 8)
        q_r, k_r = _rope(proj_b, tables, bsz, seq, heads, 512)
        lam_vecs = jnp.stack([diff_lambda_q1[l], diff_lambda_k1[l], diff_lambda_q2[l],
                              diff_lambda_k2[l]]).astype(_F32)
        lam_init = 0.8 - 0.6 * math.exp(-0.3 * l)
        y_diff = _diff_attention(q_r, k_r, proj_b, lam_vecs, diff_subln_w[l].astype(_F32), lam_init,
                                 bsz, seq, heads, 512, 256)

        merged = _branch_merge(y_sba, y_gdn, y_diff, wb_sba, wb_gdn, wb_diff, l, proj_b, 3 * w, 512, 512)
        h = _outproj_residual(merged, w_out16, l, x32, alpha, 1024, 512)
        x32, x16 = _layer_norm(h, ln1_g[l], ln1_b[l], 256)

        x32, x16 = _moe_layer(x32, w_router_group[l], b_router_group[l], w_router_expert[l],
                              b_router_expert[l], w_gate16, w_up16, w_down16, l,
                              ln2_g[l], ln2_b[l], alpha, 512, tm_e, 128)
    return x32.reshape(bsz, seq, d)
```

```python
import functools
import math

import jax
import jax.numpy as jnp
from jax import lax
from jax.experimental import pallas as pl
from jax.experimental.pallas import tpu as pltpu

_F32 = jnp.float32
_BF16 = jnp.bfloat16

_LANES = 128
_VMEM_LIMIT = 56 * 1024 * 1024
_HEAD_DIM = 128
_DIFF_DH = 64
_GDN_CHUNK = 64
_GDN_SUB = 16
_CONV_K = 4
_ROPE_THETA = 10000.0
_N_GROUPS = 4
_EXPERTS_PER_GROUP = 8
_N_EXPERTS = _N_GROUPS * _EXPERTS_PER_GROUP
_LN_EPS = 1e-5
_NORM_EPS = 1e-6
_NEG = -1e30
_LOG2E = 1.4426950408889634


def _params(sem):
    return pltpu.CompilerParams(dimension_semantics=sem, vmem_limit_bytes=_VMEM_LIMIT)


def _nt_dot(a, b):
    return lax.dot_general(a, b, (((1,), (1,)), ((), ())), preferred_element_type=_F32)


def _dot(a, b):
    return jnp.dot(a, b, preferred_element_type=_F32)


def _split2(a):
    hi = a.astype(_BF16)
    lo = (a - hi.astype(_F32)).astype(_BF16)
    return hi, lo


def _dot_x3(a, b, nt=False):
    f = _nt_dot if nt else _dot
    ah, al = _split2(a)
    bh, bl = _split2(b)
    return f(ah, bh) + (f(ah, bl) + f(al, bh))


def _dot_exact_lhs(a_bf16, b, nt=False):
    f = _nt_dot if nt else _dot
    b1 = b.astype(_BF16)
    r1 = b - b1.astype(_F32)
    b2 = r1.astype(_BF16)
    b3 = (r1 - b2.astype(_F32)).astype(_BF16)
    return f(a_bf16, b1) + (f(a_bf16, b2) + f(a_bf16, b3))


def _sigmoid(x):
    return 1.0 / (1.0 + jnp.exp(-x))


def _softplus(x):
    return jnp.maximum(x, 0.0) + jnp.log(1.0 + jnp.exp(-jnp.abs(x)))


def _mm_kernel(x_ref, w_ref, o_ref):
    o_ref[...] = _nt_dot(x_ref[...], w_ref[0]).astype(o_ref.dtype)


def _matmul_nt(x, w, layer, row0, n, out_dtype, tm, tn, name):
    m, k = x.shape
    tm, tn = min(tm, m), min(tn, n)
    while n % tn:
        tn //= 2
    return pl.pallas_call(
        _mm_kernel,
        out_shape=jax.ShapeDtypeStruct((m, n), out_dtype),
        grid=(m // tm, n // tn),
        in_specs=[pl.BlockSpec((tm, k), lambda i, j: (i, 0)),
                  pl.BlockSpec((pl.Element(1), pl.Element(tn), pl.Element(k)),
                               lambda i, j: (layer, pl.multiple_of(row0 + j * tn, math.gcd(row0, tn)), 0))],
        out_specs=pl.BlockSpec((tm, tn), lambda i, j: (i, j)),
        compiler_params=_params(("parallel", "arbitrary")),
        name=name,
    )(x, w)


def _merge_kernel(ys_ref, yg_ref, yd_ref, ws_ref, wg_ref, wd_ref, gs_ref, gg_ref, gd_ref, o_ref):
    acc = _sigmoid(gs_ref[...].astype(_F32)) * _dot(ys_ref[...], ws_ref[...])
    acc += _sigmoid(gg_ref[...].astype(_F32)) * _dot(yg_ref[...], wg_ref[...])
    acc += _sigmoid(gd_ref[...].astype(_F32)) * _dot(yd_ref[...], wd_ref[...])
    o_ref[...] = acc.astype(o_ref.dtype)


def _branch_merge(y_sba, y_gdn, y_diff, wb_sba, wb_gdn, wb_diff, layer, proj, gate_col0, tm, tn):
    t, w = y_sba.shape
    d = wb_sba.shape[2]
    tm, tn = min(tm, t), min(tn, d)
    g0 = gate_col0 // tn
    nd = d // tn
    y_spec = pl.BlockSpec((tm, w), lambda i, j: (i, 0))
    w_spec = pl.BlockSpec((None, w, tn), lambda i, j: (layer, 0, j))

    def gate_spec(b):
        return pl.BlockSpec((tm, tn), lambda i, j: (i, g0 + b * nd + j))

    return pl.pallas_call(
        _merge_kernel,
        out_shape=jax.ShapeDtypeStruct((t, d), _BF16),
        grid=(t // tm, nd),
        in_specs=[y_spec, y_spec, y_spec, w_spec, w_spec, w_spec,
                  gate_spec(0), gate_spec(1), gate_spec(2)],
        out_specs=pl.BlockSpec((tm, tn), lambda i, j: (i, j)),
        compiler_params=_params(("parallel", "arbitrary")),
        name="branch_merge",
    )(y_sba, y_gdn, y_diff, wb_sba, wb_gdn, wb_diff, proj, proj, proj)


def _outproj_kernel(m_ref, w_ref, x_ref, o_ref, *, alpha):
    o_ref[...] = alpha * x_ref[...] + _dot(m_ref[...], w_ref[...])


def _outproj_residual(merged, w_out, layer, x, alpha, tm, tn):
    t, d = merged.shape
    tm, tn = min(tm, t), min(tn, d)
    return pl.pallas_call(
        functools.partial(_outproj_kernel, alpha=alpha),
        out_shape=jax.ShapeDtypeStruct((t, d), _F32),
        grid=(t // tm, d // tn),
        in_specs=[pl.BlockSpec((tm, d), lambda i, j: (i, 0)),
                  pl.BlockSpec((None, d, tn), lambda i, j: (layer, 0, j)),
                  pl.BlockSpec((tm, tn), lambda i, j: (i, j))],
        out_specs=pl.BlockSpec((tm, tn), lambda i, j: (i, j)),
        compiler_params=_params(("parallel", "arbitrary")),
        name="outproj_residual",
    )(merged, w_out, x)


def _layer_norm_rows(y, g, b):
    mu = jnp.mean(y, axis=-1, keepdims=True)
    yc = y - mu
    var = jnp.mean(yc * yc, axis=-1, keepdims=True)
    return yc * lax.rsqrt(var + _LN_EPS) * g + b


def _ln_kernel(y_ref, g_ref, b_ref, o32_ref, o16_ref):
    out = _layer_norm_rows(y_ref[...], g_ref[...], b_ref[...])
    o32_ref[...] = out
    o16_ref[...] = out.astype(_BF16)


def _layer_norm(y, g, b, tr):
    t, d = y.shape
    tr = min(tr, t)
    row = pl.BlockSpec((tr, d), lambda i: (i, 0))
    vec = pl.BlockSpec((1, d), lambda i: (0, 0))
    return pl.pallas_call(
        _ln_kernel,
        out_shape=(jax.ShapeDtypeStruct((t, d), _F32), jax.ShapeDtypeStruct((t, d), _BF16)),
        grid=(t // tr,),
        in_specs=[row, vec, vec],
        out_specs=(row, row),
        compiler_params=_params(("parallel",)),
        name="layer_norm",
    )(y, g.reshape(1, d), b.reshape(1, d))


def _neg_abs(x):
    bits = lax.bitcast_convert_type(x, jnp.uint32) | jnp.uint32(0x80000000)
    return lax.bitcast_convert_type(bits, _F32)


_SWEEP_UNROLLS = (8, 4, 2)


def _causal_sweep(run, n_blocks, carry):
    done = 0
    for un in _SWEEP_UNROLLS:
        trips = (n_blocks - done) // un
        first = n_blocks - 1 - done
        carry = lax.fori_loop(0, trips, lambda i, c, un=un, first=first: run(first - un * i, c, un, False),
                              carry)
        done = done + trips * un
    return carry


def _sba_kernel(q_ref, k_ref, v_ref, u_ref, o_ref, acc_ref, w_ref, *, tq, tk, scale):
    qi = pl.program_id(2)
    q = (q_ref[...].astype(_F32) * (scale * _LOG2E)).astype(_BF16)
    u = u_ref[...]
    n_diag = tq // tk
    acc_ref[...] = jnp.zeros_like(acc_ref)

    def scores(kb):
        ks = pl.multiple_of(jnp.maximum(kb, 0) * tk, tk)
        return _nt_dot(q, k_ref[pl.ds(ks, tk), :])

    def consume(w, kb, carry, masked):
        v = v_ref[pl.ds(pl.multiple_of(kb * tk, tk), tk), :]
        sp = jnp.maximum(w, 0.0) + jnp.log(1.0 + jnp.exp2(_neg_abs(w))) * _LOG2E
        if masked:
            rows = qi * tq + lax.broadcasted_iota(jnp.int32, (tq, tk), 0)
            cols = kb * tk + lax.broadcasted_iota(jnp.int32, (tq, tk), 1)
            strict = cols < rows
            sp = jnp.where(strict, sp, 0.0)
        later = _dot(sp.astype(_BF16), u)
        att = jnp.exp2(((w - sp) - later) - carry)
        if masked:
            att = jnp.where(strict, att, 0.0)
        acc_ref[...] += _dot(att.astype(_BF16), v)
        return carry + jnp.sum(sp, axis=1, keepdims=True)

    def run(kb0, carry, nblk, masked):
        for j in range(nblk):
            w_ref[(j + 1) % 2] = scores(kb0 - j - 1)
            carry = consume(w_ref[j % 2], kb0 - j, carry, masked)
        return carry

    top = (qi + 1) * n_diag - 1
    w_ref[0] = scores(top)
    carry = run(top, jnp.zeros((tq, 1), _F32), n_diag, True)
    n_full = qi * n_diag
    _causal_sweep(run, n_full, carry)
    o_ref[...] = acc_ref[...].astype(o_ref.dtype)


def _stick_breaking(proj, bsz, seq, heads, tq, tk):
    t = bsz * seq
    tq, tk = min(tq, seq), min(tk, seq)
    tk = min(tk, tq)
    nq = seq // tq
    assert (tq // tk) % 2 == 0, "the two score slots alternate per key block"
    ids = lax.broadcasted_iota(jnp.int32, (tk, tk), 0)
    u = (ids > ids.T).astype(_BF16)
    return pl.pallas_call(
        functools.partial(_sba_kernel, tq=tq, tk=tk, scale=_HEAD_DIM ** -0.5),
        out_shape=jax.ShapeDtypeStruct((t, heads * _HEAD_DIM), _BF16),
        grid=(bsz, heads, nq),
        in_specs=[pl.BlockSpec((tq, _HEAD_DIM), lambda b, h, i: (b * nq + i, h)),
                  pl.BlockSpec((seq, _HEAD_DIM), lambda b, h, i: (b, heads + h)),
                  pl.BlockSpec((seq, _HEAD_DIM), lambda b, h, i: (b, 2 * heads + h)),
                  pl.BlockSpec((tk, tk), lambda b, h, i: (0, 0))],
        out_specs=pl.BlockSpec((tq, _HEAD_DIM), lambda b, h, i: (b * nq + i, h)),
        scratch_shapes=[pltpu.VMEM((tq, _HEAD_DIM), _F32), pltpu.VMEM((2, tq, tk), _F32)],
        compiler_params=_params(("parallel", "parallel", "arbitrary")),
        name="stick_breaking_attention",
    )(proj, proj, proj, u)


def _rope_kernel(q_ref, k_ref, cos_ref, sa_ref, sb_ref, qo_ref, ko_ref, *, heads, scale):
    cos, sa, sb = cos_ref[...], sa_ref[...], sb_ref[...]
    for h in range(heads):
        sl = slice(h * _HEAD_DIM, (h + 1) * _HEAD_DIM)
        for src, dst, s in ((q_ref, qo_ref, scale), (k_ref, ko_ref, 1.0)):
            x = src[:, sl].astype(_F32)
            r = (x * cos + pltpu.roll(x, _HEAD_DIM - _DIFF_DH // 2, axis=1) * sa
                 + pltpu.roll(x, _DIFF_DH // 2, axis=1) * sb)
            dst[:, sl] = (r * s).astype(dst.dtype)


def _rope_tables(seq):
    half = _DIFF_DH // 2
    pos = jnp.arange(seq, dtype=_F32)
    inv_freq = _ROPE_THETA ** (-jnp.arange(0, _DIFF_DH, 2, dtype=_F32) / _DIFF_DH)
    ang = pos[:, None] * inv_freq[None, :]
    cos, sin, zero = jnp.cos(ang), jnp.sin(ang), jnp.zeros_like(ang)
    cos_t = jnp.concatenate([cos] * 4, axis=-1)
    sa_t = jnp.concatenate([-sin, zero] * 2, axis=-1)
    sb_t = jnp.concatenate([zero, sin] * 2, axis=-1)
    return cos_t, sa_t, sb_t


def _rope(proj, tables, bsz, seq, heads, ts):
    t = bsz * seq
    w = heads * _HEAD_DIM
    ts = min(ts, seq)
    ns = seq // ts
    tab = pl.BlockSpec((ts, _HEAD_DIM), lambda i: (i % ns, 0))
    out = pl.BlockSpec((ts, w), lambda i: (i, 0))
    return pl.pallas_call(
        functools.partial(_rope_kernel, heads=heads, scale=_DIFF_DH ** -0.5 * _LOG2E),
        out_shape=(jax.ShapeDtypeStruct((t, w), _BF16), jax.ShapeDtypeStruct((t, w), _BF16)),
        grid=(t // ts,),
        in_specs=[pl.BlockSpec((ts, w), lambda i: (i, 0)),
                  pl.BlockSpec((ts, w), lambda i: (i, 1)), tab, tab, tab],
        out_specs=(out, out),
        compiler_params=_params(("parallel",)),
        name="diff_rope",
    )(proj, proj, *tables)


def _diff_kernel(lam_ref, sw_ref, q_ref, k_ref, v_ref, o_ref, acc_ref, s_ref, *, tq, tk, lam_init):
    qi = pl.program_id(2)
    q = q_ref[...]
    lane = lax.broadcasted_iota(jnp.int32, (tq, _HEAD_DIM), 1)
    zero = jnp.zeros_like(q)
    qs = (jnp.where(lane < _DIFF_DH, q, zero), jnp.where(lane >= _DIFF_DH, q, zero))
    n_diag = tq // tk
    acc_ref[...] = jnp.zeros_like(acc_ref)

    def put_scores(slot, kb):
        ks = pl.multiple_of(jnp.maximum(kb, 0) * tk, tk)
        k = k_ref[pl.ds(ks, tk), :]
        for c in range(2):
            s_ref[slot, c] = _nt_dot(k, qs[c])

    def consume(slot, kb, carry, masked):
        v = v_ref[pl.ds(pl.multiple_of(kb * tk, tk), tk), :]
        if masked:
            keys = kb * tk + lax.broadcasted_iota(jnp.int32, (tk, tq), 0)
            qpos = qi * tq + lax.broadcasted_iota(jnp.int32, (tk, tq), 1)
            causal = keys <= qpos
        out = []
        for c in range(2):
            m_prev, l_prev = carry[2 * c], carry[2 * c + 1]
            s = s_ref[slot, c]
            if masked:
                s = jnp.where(causal, s, _NEG)
            m_new = jnp.maximum(m_prev, jnp.max(s, axis=0, keepdims=True))
            p = jnp.exp2(s - m_new)
            alpha = jnp.exp2(m_prev - m_new)
            pv = lax.dot_general(v, p.astype(_BF16), (((0,), (0,)), ((), ())), preferred_element_type=_F32)
            acc_ref[c] = alpha * acc_ref[c] + pv
            out += [m_new, alpha * l_prev + jnp.sum(p, axis=0, keepdims=True)]
        return tuple(out)

    def run(kb0, carry, nblk, masked):
        for j in range(nblk):
            put_scores((j + 1) % 2, kb0 - j - 1)
            carry = consume(j % 2, kb0 - j, carry, masked)
        return carry

    neg = jnp.full((1, tq), _NEG, _F32)
    zero_row = jnp.zeros((1, tq), _F32)
    top = (qi + 1) * n_diag - 1
    put_scores(0, top)
    carry = run(top, (neg, zero_row, neg, zero_row), n_diag, True)
    n_full = qi * n_diag
    carry = _causal_sweep(run, n_full, carry)

    lv = lam_ref[...]
    lam = (jnp.exp(jnp.sum(lv[0:1] * lv[1:2], axis=1, keepdims=True))
           - jnp.exp(jnp.sum(lv[2:3] * lv[3:4], axis=1, keepdims=True)) + lam_init)
    o = acc_ref[0] / carry[1] - lam * (acc_ref[1] / carry[3])
    o = o * lax.rsqrt(jnp.mean(o * o, axis=0, keepdims=True) + _LN_EPS)
    o_ref[...] = (o.T * sw_ref[...] * (1.0 - lam_init)).astype(o_ref.dtype)


def _diff_attention(q_r, k_r, proj, lam_vecs, subln_w, lam_init, bsz, seq, heads, tq, tk):
    t = bsz * seq
    tq, tk = min(tq, seq), min(tk, seq)
    tk = min(tk, tq)
    nq = seq // tq
    return pl.pallas_call(
        functools.partial(_diff_kernel, tq=tq, tk=tk, lam_init=lam_init),
        out_shape=jax.ShapeDtypeStruct((t, heads * _HEAD_DIM), _BF16),
        grid=(bsz, heads, nq),
        in_specs=[pl.BlockSpec((4, _DIFF_DH), lambda b, h, i: (0, 0)),
                  pl.BlockSpec((1, _HEAD_DIM), lambda b, h, i: (0, 0)),
                  pl.BlockSpec((tq, _HEAD_DIM), lambda b, h, i: (b * nq + i, h)),
                  pl.BlockSpec((seq, _HEAD_DIM), lambda b, h, i: (b, h)),
                  pl.BlockSpec((seq, _HEAD_DIM), lambda b, h, i: (b, 2 * heads + h))],
        out_specs=pl.BlockSpec((tq, _HEAD_DIM), lambda b, h, i: (b * nq + i, h)),
        scratch_shapes=[pltpu.VMEM((2, _HEAD_DIM, tq), _F32), pltpu.VMEM((2, 2, tk, tq), _F32)],
        compiler_params=_params(("parallel", "parallel", "arbitrary")),
        name="differential_attention",
    )(lam_vecs, subln_w.reshape(1, _HEAD_DIM), q_r, k_r, proj)


def _bdot(a, b):
    return lax.dot_general(a, b, (((2,), (1,)), ((0,), (0,))), preferred_element_type=_F32)


def _bdot_nt(a, b):
    return lax.dot_general(a, b, (((2,), (2,)), ((0,), (0,))), preferred_element_type=_F32)


def _bdot_tn(a, b):
    return lax.dot_general(a, b, (((1,), (1,)), ((0,), (0,))), preferred_element_type=_F32)


def _bdot_x3(a, b):
    ah, al = _split2(a)
    bh, bl = _split2(b)
    return _bdot(ah, bh) + (_bdot(ah, bl) + _bdot(al, bh))


def _unit_lower_inverse(lm, eye, blockdiag):
    c = lm.shape[-1]
    md = jnp.where(blockdiag, -lm, 0.0)
    x = eye + md
    p = _bdot_x3(md, md)
    steps = int(math.log2(_GDN_SUB)) - 1
    for it in range(steps):
        x = x + _bdot_x3(x, p)
        if it + 1 < steps:
            p = _bdot_x3(p, p)
    n = _bdot_x3(x, jnp.where(blockdiag, 0.0, lm))
    y = eye - n
    pw = n
    for _ in range(int(math.log2(c // _GDN_SUB)) - 1):
        pw = _bdot_x3(pw, pw)
        y = y + _bdot_x3(y, pw)
    return _bdot_x3(y, x)


def _gdn_kernel(ab_ref, q_ref, k_ref, v_ref, z_ref, wq_ref, wk_ref, wv_ref, alog_ref, dtb_ref, nw_ref,
                tri_ref, o_ref, xp_ref, qs_ref, ks_ref, vs_ref, dm_ref, gc_ref, bt_ref,
                u_ref, w_ref, qd_ref, kd_ref, in_ref, dl_ref, state_ref, *, tb, gh):
    i = pl.program_id(2)
    c = _GDN_CHUNK
    nc = tb // c
    halo = 8

    @pl.when(i == 0)
    def _():
        xp_ref[:, 0:halo, :] = jnp.zeros((3, halo, gh * _HEAD_DIM), _F32)
        state_ref[...] = jnp.zeros_like(state_ref)

    @pl.when(i > 0)
    def _():
        xp_ref[:, 0:halo, :] = xp_ref[:, tb:tb + halo, :]

    for idx, (src, cw_ref, dst) in enumerate(((q_ref, wq_ref, qs_ref), (k_ref, wk_ref, ks_ref),
                                              (v_ref, wv_ref, vs_ref))):
        xp_ref[idx, halo:halo + tb, :] = src[...].astype(_F32)
        w = cw_ref[...]
        y = xp_ref[idx, halo - 3:halo - 3 + tb, :] * w[0:1, :]
        for j in range(1, _CONV_K):
            y = y + xp_ref[idx, halo - 3 + j:halo - 3 + j + tb, :] * w[j:j + 1, :]
        y = y * _sigmoid(y)
        for g in range(gh):
            yh = y[:, g * _HEAD_DIM:(g + 1) * _HEAD_DIM]
            if idx < 2:
                yh = yh * lax.rsqrt(jnp.sum(yh * yh, axis=1, keepdims=True) + _NORM_EPS)
            if idx == 0:
                yh = yh * (_HEAD_DIM ** -0.5)
            for ci in range(nc):
                dst[ci * gh + g] = yh[ci * c:(ci + 1) * c, :]

    ab = ab_ref[0]
    g_all = -jnp.exp(alog_ref[0]) * _softplus(ab + dtb_ref[0])
    beta_all = _sigmoid(ab)
    tri = tri_ref[...]
    pick = (lax.broadcasted_iota(jnp.int32, (8, _LANES), 0)
            == lax.broadcasted_iota(jnp.int32, (8, _LANES), 1)).astype(_BF16)
    for ci in range(nc):
        rows = slice(ci * c, (ci + 1) * c)
        gcum = _dot_exact_lhs(tri, g_all[rows, :])
        gcum_t = _dot_exact_lhs(pick, gcum, nt=True)
        for g in range(gh):
            gcol = gcum[:, g:g + 1]
            dm_ref[ci * gh + g] = gcol - gcum_t[g:g + 1, :]
            gc_ref[ci * gh + g] = jnp.broadcast_to(gcol, (c, _LANES))
            bt_ref[ci * gh + g] = jnp.broadcast_to(beta_all[rows, gh + g:gh + g + 1], (c, _LANES))

    ri = lax.broadcasted_iota(jnp.int32, (c, c), 0)
    cj = lax.broadcasted_iota(jnp.int32, (c, c), 1)
    incl, strict = ri >= cj, ri > cj
    sub_shift = int(math.log2(_GDN_SUB))
    blockdiag = lax.shift_right_logical(ri, sub_shift) == lax.shift_right_logical(cj, sub_shift)
    eye = (ri == cj).astype(_F32)

    gc = gc_ref[...]
    beta = bt_ref[...]
    decay = jnp.where(incl, jnp.exp(jnp.where(incl, dm_ref[...], 0.0)), 0.0)
    g_last = gc[:, c - 1:c, :]
    eg = jnp.exp(gc)
    dl_ref[...] = jnp.exp(g_last)
    q, k, v = qs_ref[...], ks_ref[...], vs_ref[...]
    k16 = k.astype(_BF16)
    kb = k * beta
    lm = jnp.where(strict, _bdot_nt(kb.astype(_BF16), k16) * decay, 0.0)
    tinv = _unit_lower_inverse(lm, eye, blockdiag)
    u_ref[...] = _bdot_x3(tinv, v * beta)
    w_ref[...] = _bdot_x3(tinv, kb * eg).astype(_BF16)
    in_ref[...] = (_bdot_nt(q.astype(_BF16), k16) * decay).astype(_BF16)
    qd_ref[...] = (q * eg).astype(_BF16)
    kd_ref[...] = (k * jnp.exp(g_last - gc)).astype(_BF16)

    nw = nw_ref[...]

    def chunk(cidx, _):
        sl = pl.ds(cidx * gh, gh)
        st = state_ref[...]
        st16 = st.astype(_BF16)
        v_new = u_ref[sl] - _bdot(w_ref[sl], st16)
        vn16 = v_new.astype(_BF16)
        o = _bdot(qd_ref[sl], st16) + _bdot(in_ref[sl], vn16)
        state_ref[...] = st * dl_ref[sl] + _bdot_tn(kd_ref[sl], vn16)
        o = o * lax.rsqrt(jnp.mean(o * o, axis=2, keepdims=True) + _NORM_EPS) * nw
        rows = pl.ds(pl.multiple_of(cidx * c, c), c)
        for g in range(gh):
            lanes = slice(g * _HEAD_DIM, (g + 1) * _HEAD_DIM)
            zf = z_ref[rows, lanes].astype(_F32)
            o_ref[rows, lanes] = (o[g] * (zf * _sigmoid(zf))).astype(o_ref.dtype)
        return 0

    lax.fori_loop(0, nc, chunk, 0)


def _gated_deltanet(proj, ab, conv_w, a_log, dt_bias, norm_w, bsz, seq, heads, tb, gh):
    t = bsz * seq
    w = heads * _HEAD_DIM
    tb = min(tb, seq)
    gh = min(gh, heads)
    assert gh <= 8, "one sublane tile holds the transposed per-head cumulative gates"
    nb = seq // tb
    ng = heads // gh
    c = _GDN_CHUNK
    nbatch = (tb // c) * gh
    ids = lax.broadcasted_iota(jnp.int32, (c, c), 0)
    tri = (ids >= ids.T).astype(_BF16)

    def group(x):
        return jnp.moveaxis(x.reshape(x.shape[0], ng, gh), 1, 0)

    lane_pad = ((0, 0), (0, 0), (0, _LANES - 2 * gh))
    ab_g = jnp.pad(jnp.concatenate([group(ab[:, :heads]), group(ab[:, heads:2 * heads])], axis=2), lane_pad)
    zeros = jnp.zeros((1, heads), _F32)
    alog_g = jnp.pad(jnp.concatenate([group(a_log.astype(_F32)[None]), group(zeros)], axis=2), lane_pad)
    dtb_g = jnp.pad(jnp.concatenate([group(dt_bias.astype(_F32)[None]), group(zeros)], axis=2), lane_pad)
    blk = (tb, gh * _HEAD_DIM)

    def col(base):
        return pl.BlockSpec(blk, lambda b, g, i: (b * nb + i, base * ng + g))

    def wcol(base):
        return pl.BlockSpec((_CONV_K, gh * _HEAD_DIM), lambda b, g, i: (0, base * ng + g))

    vec = pl.BlockSpec((1, 1, _LANES), lambda b, g, i: (g, 0, 0))
    batch_f32 = pltpu.VMEM((nbatch, c, _HEAD_DIM), _F32)
    batch_b16 = pltpu.VMEM((nbatch, c, _HEAD_DIM), _BF16)
    return pl.pallas_call(
        functools.partial(_gdn_kernel, tb=tb, gh=gh),
        out_shape=jax.ShapeDtypeStruct((t, w), _BF16),
        grid=(bsz, ng, nb),
        in_specs=[pl.BlockSpec((1, tb, _LANES), lambda b, g, i: (g, b * nb + i, 0)),
                  col(3), col(4), col(5), col(6), wcol(0), wcol(1), wcol(2), vec, vec,
                  pl.BlockSpec((1, _HEAD_DIM), lambda b, g, i: (0, 0)),
                  pl.BlockSpec((c, c), lambda b, g, i: (0, 0))],
        out_specs=pl.BlockSpec(blk, lambda b, g, i: (b * nb + i, g)),
        scratch_shapes=[pltpu.VMEM((3, tb + 8, gh * _HEAD_DIM), _F32),
                        batch_f32, batch_f32, batch_f32,
                        pltpu.VMEM((nbatch, c, c), _F32), batch_f32, batch_f32,
                        batch_f32, batch_b16, batch_b16, batch_b16,
                        pltpu.VMEM((nbatch, c, c), _BF16),
                        pltpu.VMEM((nbatch, 1, _LANES), _F32),
                        pltpu.VMEM((gh, _HEAD_DIM, _HEAD_DIM), _F32)],
        compiler_params=_params(("parallel", "parallel", "arbitrary")),
        name="gated_deltanet",
    )(ab_g, proj, proj, proj, proj, conv_w, conv_w, conv_w, alog_g, dtb_g,
      norm_w.reshape(1, _HEAD_DIM).astype(_F32), tri)


def _router_kernel(x_ref, w_ref, b_ref, tri_ref, route_ref, counts_ref, carry_ref, *, tm):
    i = pl.program_id(0)

    @pl.when(i == 0)
    def _():
        carry_ref[...] = jnp.zeros_like(carry_ref)

    logits = _dot_x3(x_ref[...], w_ref[...]) + b_ref[...]
    lane = lax.broadcasted_iota(jnp.int32, (tm, _LANES), 1)
    big = jnp.int32(_LANES)

    def top(vals):
        m = jnp.max(vals, axis=1, keepdims=True)
        idx = jnp.min(jnp.where(vals == m, lane, big), axis=1, keepdims=True)
        return m, idx

    gl = jnp.where(lane < _N_GROUPS, logits, _NEG)
    gm, gidx = top(gl)
    g_w = 1.0 / jnp.sum(jnp.exp(gl - gm), axis=1, keepdims=True)
    lane_group = lax.shift_right_arithmetic(lane - _N_GROUPS, int(math.log2(_EXPERTS_PER_GROUP)))
    el = jnp.where(lane_group == gidx, logits, _NEG)
    m1, i1 = top(el)
    m2, i2 = top(jnp.where(lane == i1, _NEG, el))
    r = jnp.exp(m2 - m1)
    w1 = g_w / (1.0 + r)
    w2 = g_w * r / (1.0 + r)
    e1, e2 = i1 - _N_GROUPS, i2 - _N_GROUPS
    oh1 = (lane == e1).astype(_F32)
    oh2 = (lane == e2).astype(_F32)
    both = oh1 + oh2
    before = _dot(tri_ref[...], both.astype(_BF16)) + carry_ref[0:1, :]
    rank1 = jnp.sum(oh1 * before, axis=1, keepdims=True)
    rank2 = jnp.sum(oh2 * before, axis=1, keepdims=True)
    out = jnp.zeros((tm, _LANES), _F32)
    for pos, val in enumerate((e1.astype(_F32), e2.astype(_F32), w1, w2, rank1, rank2)):
        out = jnp.where(lane == pos, val, out)
    route_ref[...] = out
    total = carry_ref[0:1, :] + jnp.sum(both, axis=0, keepdims=True)
    carry_ref[...] = jnp.broadcast_to(total, carry_ref.shape)
    counts_ref[...] = carry_ref[...]


def _router(x32, w_r, b_r, tm):
    t, d = x32.shape
    tm = min(tm, t)
    ids = lax.broadcasted_iota(jnp.int32, (tm, tm), 0)
    tri = (ids > ids.T).astype(_BF16)
    return pl.pallas_call(
        functools.partial(_router_kernel, tm=tm),
        out_shape=(jax.ShapeDtypeStruct((t, _LANES), _F32), jax.ShapeDtypeStruct((8, _LANES), _F32)),
        grid=(t // tm,),
        in_specs=[pl.BlockSpec((tm, d), lambda i: (i, 0)),
                  pl.BlockSpec((d, _LANES), lambda i: (0, 0)),
                  pl.BlockSpec((1, _LANES), lambda i: (0, 0)),
                  pl.BlockSpec((tm, tm), lambda i: (0, 0))],
        out_specs=(pl.BlockSpec((tm, _LANES), lambda i: (i, 0)),
                   pl.BlockSpec((8, _LANES), lambda i: (0, 0))),
        scratch_shapes=[pltpu.VMEM((8, _LANES), _F32)],
        compiler_params=_params(("arbitrary",)),
        name="moe_router",
    )(x32, w_r, b_r, tri)


_GATHER_UNROLL = 8


def _gather_rows(src_hbm, row_of, dst_ref, sem, n):
    def issue(r, _):
        pltpu.make_async_copy(src_hbm.at[pl.ds(row_of(r), 1)], dst_ref.at[pl.ds(r, 1)], sem).start()
        return 0

    lax.fori_loop(0, n, issue, 0, unroll=_GATHER_UNROLL)


def _wait_rows(src_hbm, dst_ref, sem, n):
    pltpu.make_async_copy(src_hbm.at[pl.ds(0, n)], dst_ref, sem).wait()


def _expert_kernel(te_ref, nact_ref, tok_ref, x_hbm, wg_ref, wu_ref, wd_ref, o_ref, buf_ref, sems, *, tm):
    i = pl.program_id(0)
    nact = nact_ref[0]
    slot = lax.rem(i, 2)

    def start(tile, s):
        base = tile * tm
        _gather_rows(x_hbm, lambda r: tok_ref[base + r], buf_ref.at[s], sems.at[s], tm)

    @pl.when(i == 0)
    def _():
        start(0, 0)

    @pl.when(i + 1 < nact)
    def _():
        start(i + 1, 1 - slot)

    @pl.when(i < nact)
    def _():
        _wait_rows(x_hbm, buf_ref.at[slot], sems.at[slot], tm)
        x = buf_ref[slot].astype(_BF16)
        hg = _dot(x, wg_ref[0])
        hu = _dot(x, wu_ref[0])
        h = (hg * _sigmoid(hg)) * hu
        o_ref[...] = _dot(h.astype(_BF16), wd_ref[0])

    @pl.when(i >= nact)
    def _():
        o_ref[...] = jnp.zeros_like(o_ref)


def _expert_ffn(x32, tok_of_slot, w_gate, w_up, w_down, layer, tile_expert, nact, n_tiles, tm):
    t, d = x32.shape
    f = w_gate.shape[3]
    return pl.pallas_call(
        functools.partial(_expert_kernel, tm=tm),
        out_shape=jax.ShapeDtypeStruct((n_tiles * tm, d), _F32),
        grid_spec=pltpu.PrefetchScalarGridSpec(
            num_scalar_prefetch=3,
            grid=(n_tiles,),
            in_specs=[pl.BlockSpec(memory_space=pl.ANY),
                      pl.BlockSpec((None, 1, d, f), lambda i, te, na, tok: (layer, te[i], 0, 0)),
                      pl.BlockSpec((None, 1, d, f), lambda i, te, na, tok: (layer, te[i], 0, 0)),
                      pl.BlockSpec((None, 1, f, d), lambda i, te, na, tok: (layer, te[i], 0, 0))],
            out_specs=pl.BlockSpec((tm, d), lambda i, te, na, tok: (i, 0)),
            scratch_shapes=[pltpu.VMEM((2, tm, d), _F32), pltpu.SemaphoreType.DMA((2,))]),
        compiler_params=_params(("arbitrary",)),
        name="moe_expert_ffn",
    )(tile_expert, nact, tok_of_slot, x32, w_gate, w_up, w_down)


def _combine_kernel(pos_ref, ys_hbm, route_ref, x_ref, g_ref, b_ref, o32_ref, o16_ref, buf_ref, sems,
                    *, tc, alpha):
    i = pl.program_id(0)
    slot = lax.rem(i, 2)

    def start(step, s):
        base = step * tc
        for k in range(2):
            _gather_rows(ys_hbm, lambda r, k=k: pos_ref[2 * (base + r) + k], buf_ref.at[s, k],
                         sems.at[s, k], tc)

    @pl.when(i == 0)
    def _():
        start(0, 0)

    @pl.when(i + 1 < pl.num_programs(0))
    def _():
        start(i + 1, 1 - slot)

    for k in range(2):
        _wait_rows(ys_hbm, buf_ref.at[slot, k], sems.at[slot, k], tc)
    route = route_ref[...]
    w1, w2 = route[:, 2:3], route[:, 3:4]
    y = alpha * x_ref[...] + (w1 * buf_ref[slot, 0] + w2 * buf_ref[slot, 1])
    out = _layer_norm_rows(y, g_ref[...], b_ref[...])
    o32_ref[...] = out
    o16_ref[...] = out.astype(_BF16)


def _combine_norm(ys, pos_flat, route, x32, g, b, alpha, tc):
    t, d = x32.shape
    tc = min(tc, t)
    row = lambda: pl.BlockSpec((tc, d), lambda i, pos: (i, 0))
    vec = lambda: pl.BlockSpec((1, d), lambda i, pos: (0, 0))
    return pl.pallas_call(
        functools.partial(_combine_kernel, tc=tc, alpha=alpha),
        out_shape=(jax.ShapeDtypeStruct((t, d), _F32), jax.ShapeDtypeStruct((t, d), _BF16)),
        grid_spec=pltpu.PrefetchScalarGridSpec(
            num_scalar_prefetch=1,
            grid=(t // tc,),
            in_specs=[pl.BlockSpec(memory_space=pl.ANY),
                      pl.BlockSpec((tc, _LANES), lambda i, pos: (i, 0)), row(), vec(), vec()],
            out_specs=(row(), row()),
            scratch_shapes=[pltpu.VMEM((2, 2, tc, d), _F32), pltpu.SemaphoreType.DMA((2, 2))]),
        compiler_params=_params(("arbitrary",)),
        name="moe_combine_norm",
    )(pos_flat, ys, route, x32, g.reshape(1, d), b.reshape(1, d))


def _moe_layer(x32, w_rg, b_rg, w_re, b_re, w_gate, w_up, w_down, layer, ln_g, ln_b, alpha, tm_r, tm_e, tc):
    t, d = x32.shape
    pad = _LANES - _N_GROUPS - _N_EXPERTS
    w_r = jnp.pad(jnp.concatenate([w_rg, w_re], axis=1).astype(_F32), ((0, 0), (0, pad)))
    b_r = jnp.pad(jnp.concatenate([b_rg, b_re]).astype(_F32), (0, pad)).reshape(1, _LANES)
    route, counts = _router(x32, w_r, b_r, tm_r)

    counts = counts[0, :_N_EXPERTS].astype(jnp.int32)
    padded = ((counts + tm_e - 1) // tm_e) * tm_e
    ends = jnp.cumsum(padded)
    starts = ends - padded
    eid = route[:, 0:2].astype(jnp.int32)
    pos = starts[eid] + route[:, 4:6].astype(jnp.int32)
    n_tiles = (2 * t) // tm_e + _N_EXPERTS
    nact = (ends[-1] // tm_e).astype(jnp.int32).reshape(1)
    tile_start = jnp.minimum(jnp.arange(n_tiles, dtype=jnp.int32), nact[0] - 1) * tm_e
    tile_expert = jnp.minimum(jnp.sum((ends[None, :] <= tile_start[:, None]).astype(jnp.int32), axis=1),
                              _N_EXPERTS - 1)
    tok = jnp.broadcast_to(jnp.arange(t, dtype=jnp.int32)[:, None], (t, 2))
    tok_of_slot = jnp.zeros((n_tiles * tm_e,), jnp.int32).at[pos.reshape(-1)].set(tok.reshape(-1))

    ys = _expert_ffn(x32, tok_of_slot, w_gate, w_up, w_down, layer, tile_expert, nact, n_tiles, tm_e)
    return _combine_norm(ys, pos.reshape(-1), route, x32, ln_g, ln_b, alpha, tc)


def kernel(x, w_in, conv_w, gdn_a_log, gdn_dt_bias, gdn_norm_w, diff_lambda_q1, diff_lambda_k1,
           diff_lambda_q2, diff_lambda_k2, diff_subln_w, w_branch_sba, w_branch_gdn, w_branch_diff,
           w_out, ln1_g, ln1_b, w_router_group, b_router_group, w_router_expert, b_router_expert,
           w_expert_gate, w_expert_up, w_expert_down, ln2_g, ln2_b):
    bsz, seq, d = x.shape
    depth = w_in.shape[0]
    t = bsz * seq
    heads = d // 256
    w = heads * _HEAD_DIM
    alpha = (2 * depth) ** 0.25
    big = t >= 8192
    tm_e = 256 if big else 64

    tables = _rope_tables(seq)
    ab0 = 3 * w + 4 * w
    ab1 = ab0 + 2 * heads

    w_t = jnp.swapaxes(w_in, 1, 2).astype(_BF16)
    wb_sba, wb_gdn, wb_diff = (wb.astype(_BF16) for wb in (w_branch_sba, w_branch_gdn, w_branch_diff))
    w_out16 = w_out.astype(_BF16)
    w_gate16, w_up16, w_down16 = (we.astype(_BF16) for we in (w_expert_gate, w_expert_up, w_expert_down))

    x32 = x.reshape(t, d)
    x16 = x32.astype(_BF16)
    for l in range(depth):
        proj_a = _matmul_nt(x16, w_t, l, 0, ab0, _BF16, 1024, 1024, "in_proj_a")
        proj_b = _matmul_nt(x16, w_t, l, ab1, w_t.shape[1] - ab1, _BF16, 1024, 1024, "in_proj_b")
        ab = _matmul_nt(x16, w_t, l, ab0, ab1 - ab0, _F32, 1024, _LANES, "in_proj_ab")

        y_sba = _stick_breaking(proj_a, bsz, seq, heads, 512, 256)
        y_gdn = _gated_deltanet(proj_a, ab, conv_w[l], gdn_a_log[l], gdn_dt_bias[l], gdn_norm_w[l],
                                bsz, seq, heads, 256, 8)
        q_r, k_r = _rope(proj_b, tables, bsz, seq, heads, 512)
        lam_vecs = jnp.stack([diff_lambda_q1[l], diff_lambda_k1[l], diff_lambda_q2[l],
                              diff_lambda_k2[l]]).astype(_F32)
        lam_init = 0.8 - 0.6 * math.exp(-0.3 * l)
        y_diff = _diff_attention(q_r, k_r, proj_b, lam_vecs, diff_subln_w[l].astype(_F32), lam_init,
                                 bsz, seq, heads, 512, 256)

        merged = _branch_merge(y_sba, y_gdn, y_diff, wb_sba, wb_gdn, wb_diff, l, proj_b, 3 * w, 512, 512)
        h = _outproj_residual(merged, w_out16, l, x32, alpha, 1024, 512)
        x32, x16 = _layer_norm(h, ln1_g[l], ln1_b[l], 256)

        x32, x16 = _moe_layer(x32, w_router_group[l], b_router_group[l], w_router_expert[l],
                              b_router_expert[l], w_gate16, w_up16, w_down16, l,
                              ln2_g[l], ln2_b[l], alpha, 512, tm_e, 128)
    return x32.reshape(bsz, seq, d)
```

```python
import functools
import math

import jax
import jax.numpy as jnp
from jax import lax
from jax.experimental import pallas as pl
from jax.experimental.pallas import tpu as pltpu

_F32 = jnp.float32
_BF16 = jnp.bfloat16

_LANES = 128
_VMEM_LIMIT = 56 * 1024 * 1024
_HEAD_DIM = 128
_DIFF_DH = 64
_GDN_CHUNK = 64
_GDN_SUB = 16
_CONV_K = 4
_ROPE_THETA = 10000.0
_N_GROUPS = 4
_EXPERTS_PER_GROUP = 8
_N_EXPERTS = _N_GROUPS * _EXPERTS_PER_GROUP
_LN_EPS = 1e-5
_NORM_EPS = 1e-6
_NEG = -1e30
_LOG2E = 1.4426950408889634


def _params(sem):
    return pltpu.CompilerParams(dimension_semantics=sem, vmem_limit_bytes=_VMEM_LIMIT)


def _nt_dot(a, b):
    return lax.dot_general(a, b, (((1,), (1,)), ((), ())), preferred_element_type=_F32)


def _dot(a, b):
    return jnp.dot(a, b, preferred_element_type=_F32)


def _split2(a):
    hi = a.astype(_BF16)
    lo = (a - hi.astype(_F32)).astype(_BF16)
    return hi, lo


def _dot_x3(a, b, nt=False):
    f = _nt_dot if nt else _dot
    ah, al = _split2(a)
    bh, bl = _split2(b)
    return f(ah, bh) + (f(ah, bl) + f(al, bh))


def _dot_exact_lhs(a_bf16, b, nt=False):
    f = _nt_dot if nt else _dot
    b1 = b.astype(_BF16)
    r1 = b - b1.astype(_F32)
    b2 = r1.astype(_BF16)
    b3 = (r1 - b2.astype(_F32)).astype(_BF16)
    return f(a_bf16, b1) + (f(a_bf16, b2) + f(a_bf16, b3))


def _sigmoid(x):
    return 1.0 / (1.0 + jnp.exp(-x))


def _softplus(x):
    return jnp.maximum(x, 0.0) + jnp.log(1.0 + jnp.exp(-jnp.abs(x)))


def _mm_kernel(x_ref, w_ref, o_ref):
    o_ref[...] = _nt_dot(x_ref[...], w_ref[0]).astype(o_ref.dtype)


def _matmul_nt(x, w, layer, row0, n, out_dtype, tm, tn, name):
    m, k = x.shape
    tm, tn = min(tm, m), min(tn, n)
    while n % tn:
        tn //= 2
    return pl.pallas_call(
        _mm_kernel,
        out_shape=jax.ShapeDtypeStruct((m, n), out_dtype),
        grid=(m // tm, n // tn),
        in_specs=[pl.BlockSpec((tm, k), lambda i, j: (i, 0)),
                  pl.BlockSpec((pl.Element(1), pl.Element(tn), pl.Element(k)),
                               lambda i, j: (layer, pl.multiple_of(row0 + j * tn, math.gcd(row0, tn)), 0))],
        out_specs=pl.BlockSpec((tm, tn), lambda i, j: (i, j)),
        compiler_params=_params(("parallel", "arbitrary")),
        name=name,
    )(x, w)


def _merge_kernel(ys_ref, yg_ref, yd_ref, ws_ref, wg_ref, wd_ref, gs_ref, gg_ref, gd_ref, o_ref):
    acc = _sigmoid(gs_ref[...].astype(_F32)) * _dot(ys_ref[...], ws_ref[...])
    acc += _sigmoid(gg_ref[...].astype(_F32)) * _dot(yg_ref[...], wg_ref[...])
    acc += _sigmoid(gd_ref[...].astype(_F32)) * _dot(yd_ref[...], wd_ref[...])
    o_ref[...] = acc.astype(o_ref.dtype)


def _branch_merge(y_sba, y_gdn, y_diff, wb_sba, wb_gdn, wb_diff, layer, proj, gate_col0, tm, tn):
    t, w = y_sba.shape
    d = wb_sba.shape[2]
    tm, tn = min(tm, t), min(tn, d)
    g0 = gate_col0 // tn
    nd = d // tn
    y_spec = pl.BlockSpec((tm, w), lambda i, j: (i, 0))
    w_spec = pl.BlockSpec((None, w, tn), lambda i, j: (layer, 0, j))

    def gate_spec(b):
        return pl.BlockSpec((tm, tn), lambda i, j: (i, g0 + b * nd + j))

    return pl.pallas_call(
        _merge_kernel,
        out_shape=jax.ShapeDtypeStruct((t, d), _BF16),
        grid=(t // tm, nd),
        in_specs=[y_spec, y_spec, y_spec, w_spec, w_spec, w_spec,
                  gate_spec(0), gate_spec(1), gate_spec(2)],
        out_specs=pl.BlockSpec((tm, tn), lambda i, j: (i, j)),
        compiler_params=_params(("parallel", "arbitrary")),
        name="branch_merge",
    )(y_sba, y_gdn, y_diff, wb_sba, wb_gdn, wb_diff, proj, proj, proj)


def _outproj_kernel(m_ref, w_ref, x_ref, o_ref, *, alpha):
    o_ref[...] = alpha * x_ref[...] + _dot(m_ref[...], w_ref[...])


def _outproj_residual(merged, w_out, layer, x, alpha, tm, tn):
    t, d = merged.shape
    tm, tn = min(tm, t), min(tn, d)
    return pl.pallas_call(
        functools.partial(_outproj_kernel, alpha=alpha),
        out_shape=jax.ShapeDtypeStruct((t, d), _F32),
        grid=(t // tm, d // tn),
        in_specs=[pl.BlockSpec((tm, d), lambda i, j: (i, 0)),
                  pl.BlockSpec((None, d, tn), lambda i, j: (layer, 0, j)),
                  pl.BlockSpec((tm, tn), lambda i, j: (i, j))],
        out_specs=pl.BlockSpec((tm, tn), lambda i, j: (i, j)),
        compiler_params=_params(("parallel", "arbitrary")),
        name="outproj_residual",
    )(merged, w_out, x)


def _layer_norm_rows(y, g, b):
    mu = jnp.mean(y, axis=-1, keepdims=True)
    yc = y - mu
    var = jnp.mean(yc * yc, axis=-1, keepdims=True)
    return yc * lax.rsqrt(var + _LN_EPS) * g + b


def _ln_kernel(y_ref, g_ref, b_ref, o32_ref, o16_ref):
    out = _layer_norm_rows(y_ref[...], g_ref[...], b_ref[...])
    o32_ref[...] = out
    o16_ref[...] = out.astype(_BF16)


def _layer_norm(y, g, b, tr):
    t, d = y.shape
    tr = min(tr, t)
    row = pl.BlockSpec((tr, d), lambda i: (i, 0))
    vec = pl.BlockSpec((1, d), lambda i: (0, 0))
    return pl.pallas_call(
        _ln_kernel,
        out_shape=(jax.ShapeDtypeStruct((t, d), _F32), jax.ShapeDtypeStruct((t, d), _BF16)),
        grid=(t // tr,),
        in_specs=[row, vec, vec],
        out_specs=(row, row),
        compiler_params=_params(("parallel",)),
        name="layer_norm",
    )(y, g.reshape(1, d), b.reshape(1, d))


def _neg_abs(x):
    bits = lax.bitcast_convert_type(x, jnp.uint32) | jnp.uint32(0x80000000)
    return lax.bitcast_convert_type(bits, _F32)


_SWEEP_UNROLLS = (8, 4, 2)


def _causal_sweep(run, n_blocks, carry):
    done = 0
    for un in _SWEEP_UNROLLS:
        trips = (n_blocks - done) // un
        first = n_blocks - 1 - done
        carry = lax.fori_loop(0, trips, lambda i, c, un=un, first=first: run(first - un * i, c, un, False),
                              carry)
        done = done + trips * un
    return carry


def _sba_kernel(q_ref, k_ref, v_ref, u_ref, o_ref, acc_ref, w_ref, *, tq, tk, scale):
    qi = pl.program_id(2)
    q = (q_ref[...].astype(_F32) * (scale * _LOG2E)).astype(_BF16)
    u = u_ref[...]
    n_diag = tq // tk
    acc_ref[...] = jnp.zeros_like(acc_ref)

    def scores(kb):
        ks = pl.multiple_of(jnp.maximum(kb, 0) * tk, tk)
        return _nt_dot(q, k_ref[pl.ds(ks, tk), :])

    def consume(w, kb, carry, masked):
        v = v_ref[pl.ds(pl.multiple_of(kb * tk, tk), tk), :]
        sp = jnp.maximum(w, 0.0) + jnp.log(1.0 + jnp.exp2(_neg_abs(w))) * _LOG2E
        if masked:
            rows = qi * tq + lax.broadcasted_iota(jnp.int32, (tq, tk), 0)
            cols = kb * tk + lax.broadcasted_iota(jnp.int32, (tq, tk), 1)
            strict = cols < rows
            sp = jnp.where(strict, sp, 0.0)
        later = _dot(sp.astype(_BF16), u)
        att = jnp.exp2(((w - sp) - later) - carry)
        if masked:
            att = jnp.where(strict, att, 0.0)
        acc_ref[...] += _dot(att.astype(_BF16), v)
        return carry + jnp.sum(sp, axis=1, keepdims=True)

    def run(kb0, carry, nblk, masked):
        for j in range(nblk):
            w_ref[(j + 1) % 2] = scores(kb0 - j - 1)
            carry = consume(w_ref[j % 2], kb0 - j, carry, masked)
        return carry

    top = (qi + 1) * n_diag - 1
    w_ref[0] = scores(top)
    carry = run(top, jnp.zeros((tq, 1), _F32), n_diag, True)
    n_full = qi * n_diag
    _causal_sweep(run, n_full, carry)
    o_ref[...] = acc_ref[...].astype(o_ref.dtype)


def _stick_breaking(proj, bsz, seq, heads, tq, tk):
    t = bsz * seq
    tq, tk = min(tq, seq), min(tk, seq)
    tk = min(tk, tq)
    nq = seq // tq
    assert (tq // tk) % 2 == 0, "the two score slots alternate per key block"
    ids = lax.broadcasted_iota(jnp.int32, (tk, tk), 0)
    u = (ids > ids.T).astype(_BF16)
    return pl.pallas_call(
        functools.partial(_sba_kernel, tq=tq, tk=tk, scale=_HEAD_DIM ** -0.5),
        out_shape=jax.ShapeDtypeStruct((t, heads * _HEAD_DIM), _BF16),
        grid=(bsz, heads, nq),
        in_specs=[pl.BlockSpec((tq, _HEAD_DIM), lambda b, h, i: (b * nq + i, h)),
                  pl.BlockSpec((seq, _HEAD_DIM), lambda b, h, i: (b, heads + h)),
                  pl.BlockSpec((seq, _HEAD_DIM), lambda b, h, i: (b, 2 * heads + h)),
                  pl.BlockSpec((tk, tk), lambda b, h, i: (0, 0))],
        out_specs=pl.BlockSpec((tq, _HEAD_DIM), lambda b, h, i: (b * nq + i, h)),
        scratch_shapes=[pltpu.VMEM((tq, _HEAD_DIM), _F32), pltpu.VMEM((2, tq, tk), _F32)],
        compiler_params=_params(("parallel", "parallel", "arbitrary")),
        name="stick_breaking_attention",
    )(proj, proj, proj, u)


def _rope_kernel(q_ref, k_ref, cos_ref, sa_ref, sb_ref, qo_ref, ko_ref, *, heads, scale):
    cos, sa, sb = cos_ref[...], sa_ref[...], sb_ref[...]
    for h in range(heads):
        sl = slice(h * _HEAD_DIM, (h + 1) * _HEAD_DIM)
        for src, dst, s in ((q_ref, qo_ref, scale), (k_ref, ko_ref, 1.0)):
            x = src[:, sl].astype(_F32)
            r = (x * cos + pltpu.roll(x, _HEAD_DIM - _DIFF_DH // 2, axis=1) * sa
                 + pltpu.roll(x, _DIFF_DH // 2, axis=1) * sb)
            dst[:, sl] = (r * s).astype(dst.dtype)


def _rope_tables(seq):
    half = _DIFF_DH // 2
    pos = jnp.arange(seq, dtype=_F32)
    inv_freq = _ROPE_THETA ** (-jnp.arange(0, _DIFF_DH, 2, dtype=_F32) / _DIFF_DH)
    ang = pos[:, None] * inv_freq[None, :]
    cos, sin, zero = jnp.cos(ang), jnp.sin(ang), jnp.zeros_like(ang)
    cos_t = jnp.concatenate([cos] * 4, axis=-1)
    sa_t = jnp.concatenate([-sin, zero] * 2, axis=-1)
    sb_t = jnp.concatenate([zero, sin] * 2, axis=-1)
    return cos_t, sa_t, sb_t


def _rope(proj, tables, bsz, seq, heads, ts):
    t = bsz * seq
    w = heads * _HEAD_DIM
    ts = min(ts, seq)
    ns = seq // ts
    tab = pl.BlockSpec((ts, _HEAD_DIM), lambda i: (i % ns, 0))
    out = pl.BlockSpec((ts, w), lambda i: (i, 0))
    return pl.pallas_call(
        functools.partial(_rope_kernel, heads=heads, scale=_DIFF_DH ** -0.5 * _LOG2E),
        out_shape=(jax.ShapeDtypeStruct((t, w), _BF16), jax.ShapeDtypeStruct((t, w), _BF16)),
        grid=(t // ts,),
        in_specs=[pl.BlockSpec((ts, w), lambda i: (i, 0)),
                  pl.BlockSpec((ts, w), lambda i: (i, 1)), tab, tab, tab],
        out_specs=(out, out),
        compiler_params=_params(("parallel",)),
        name="diff_rope",
    )(proj, proj, *tables)


def _diff_kernel(lam_ref, sw_ref, q_ref, k_ref, v_ref, o_ref, acc_ref, s_ref, *, tq, tk, lam_init):
    qi = pl.program_id(2)
    q = q_ref[...]
    lane = lax.broadcasted_iota(jnp.int32, (tq, _HEAD_DIM), 1)
    zero = jnp.zeros_like(q)
    qs = (jnp.where(lane < _DIFF_DH, q, zero), jnp.where(lane >= _DIFF_DH, q, zero))
    n_diag = tq // tk
    acc_ref[...] = jnp.zeros_like(acc_ref)

    def put_scores(slot, kb):
        ks = pl.multiple_of(jnp.maximum(kb, 0) * tk, tk)
        k = k_ref[pl.ds(ks, tk), :]
        for c in range(2):
            s_ref[slot, c] = _nt_dot(k, qs[c])

    def consume(slot, kb, carry, masked):
        v = v_ref[pl.ds(pl.multiple_of(kb * tk, tk), tk), :]
        if masked:
            keys = kb * tk + lax.broadcasted_iota(jnp.int32, (tk, tq), 0)
            qpos = qi * tq + lax.broadcasted_iota(jnp.int32, (tk, tq), 1)
            causal = keys <= qpos
        out = []
        for c in range(2):
            m_prev, l_prev = carry[2 * c], carry[2 * c + 1]
            s = s_ref[slot, c]
            if masked:
                s = jnp.where(causal, s, _NEG)
            m_new = jnp.maximum(m_prev, jnp.max(s, axis=0, keepdims=True))
            p = jnp.exp2(s - m_new)
            alpha = jnp.exp2(m_prev - m_new)
            pv = lax.dot_general(v, p.astype(_BF16), (((0,), (0,)), ((), ())), preferred_element_type=_F32)
            acc_ref[c] = alpha * acc_ref[c] + pv
            out += [m_new, alpha * l_prev + jnp.sum(p, axis=0, keepdims=True)]
        return tuple(out)

    def run(kb0, carry, nblk, masked):
        for j in range(nblk):
            put_scores((j + 1) % 2, kb0 - j - 1)
            carry = consume(j % 2, kb0 - j, carry, masked)
        return carry

    neg = jnp.full((1, tq), _NEG, _F32)
    zero_row = jnp.zeros((1, tq), _F32)
    top = (qi + 1) * n_diag - 1
    put_scores(0, top)
    carry = run(top, (neg, zero_row, neg, zero_row), n_diag, True)
    n_full = qi * n_diag
    carry = _causal_sweep(run, n_full, carry)

    lv = lam_ref[...]
    lam = (jnp.exp(jnp.sum(lv[0:1] * lv[1:2], axis=1, keepdims=True))
           - jnp.exp(jnp.sum(lv[2:3] * lv[3:4], axis=1, keepdims=True)) + lam_init)
    o = acc_ref[0] / carry[1] - lam * (acc_ref[1] / carry[3])
    o = o * lax.rsqrt(jnp.mean(o * o, axis=0, keepdims=True) + _LN_EPS)
    o_ref[...] = (o.T * sw_ref[...] * (1.0 - lam_init)).astype(o_ref.dtype)


def _diff_attention(q_r, k_r, proj, lam_vecs, subln_w, lam_init, bsz, seq, heads, tq, tk):
    t = bsz * seq
    tq, tk = min(tq, seq), min(tk, seq)
    tk = min(tk, tq)
    nq = seq // tq
    return pl.pallas_call(
        functools.partial(_diff_kernel, tq=tq, tk=tk, lam_init=lam_init),
        out_shape=jax.ShapeDtypeStruct((t, heads * _HEAD_DIM), _BF16),
        grid=(bsz, heads, nq),
        in_specs=[pl.BlockSpec((4, _DIFF_DH), lambda b, h, i: (0, 0)),
                  pl.BlockSpec((1, _HEAD_DIM), lambda b, h, i: (0, 0)),
                  pl.BlockSpec((tq, _HEAD_DIM), lambda b, h, i: (b * nq + i, h)),
                  pl.BlockSpec((seq, _HEAD_DIM), lambda b, h, i: (b, h)),
                  pl.BlockSpec((seq, _HEAD_DIM), lambda b, h, i: (b, 2 * heads + h))],
        out_specs=pl.BlockSpec((tq, _HEAD_DIM), lambda b, h, i: (b * nq + i, h)),
        scratch_shapes=[pltpu.VMEM((2, _HEAD_DIM, tq), _F32), pltpu.VMEM((2, 2, tk, tq), _F32)],
        compiler_params=_params(("parallel", "parallel", "arbitrary")),
        name="differential_attention",
    )(lam_vecs, subln_w.reshape(1, _HEAD_DIM), q_r, k_r, proj)


def _bdot(a, b):
    return lax.dot_general(a, b, (((2,), (1,)), ((0,), (0,))), preferred_element_type=_F32)


def _bdot_nt(a, b):
    return lax.dot_general(a, b, (((2,), (2,)), ((0,), (0,))), preferred_element_type=_F32)


def _bdot_tn(a, b):
    return lax.dot_general(a, b, (((1,), (1,)), ((0,), (0,))), preferred_element_type=_F32)


def _bdot16(a, b):
    return _bdot(a.astype(_BF16), b.astype(_BF16))


def _unit_lower_inverse(lm, eye, blockdiag):
    c = lm.shape[-1]
    md = jnp.where(blockdiag, -lm, 0.0)
    x = eye + md
    p = _bdot16(md, md)
    steps = int(math.log2(_GDN_SUB)) - 1
    for it in range(steps):
        x = x + _bdot16(x, p)
        if it + 1 < steps:
            p = _bdot16(p, p)
    n = _bdot16(x, jnp.where(blockdiag, 0.0, lm))
    y = eye - n
    pw = n
    for _ in range(int(math.log2(c // _GDN_SUB)) - 1):
        pw = _bdot16(pw, pw)
        y = y + _bdot16(y, pw)
    return _bdot16(y, x)


def _gdn_kernel(ab_ref, q_ref, k_ref, v_ref, z_ref, wq_ref, wk_ref, wv_ref, alog_ref, dtb_ref, nw_ref,
                tri_ref, o_ref, xp_ref, qs_ref, ks_ref, vs_ref, dm_ref, gc_ref, bt_ref,
                u_ref, w_ref, qd_ref, kd_ref, in_ref, dl_ref, state_ref, *, tb, gh):
    i = pl.program_id(2)
    c = _GDN_CHUNK
    nc = tb // c
    halo = 8

    @pl.when(i == 0)
    def _():
        xp_ref[:, 0:halo, :] = jnp.zeros((3, halo, gh * _HEAD_DIM), _F32)
        state_ref[...] = jnp.zeros_like(state_ref)

    @pl.when(i > 0)
    def _():
        xp_ref[:, 0:halo, :] = xp_ref[:, tb:tb + halo, :]

    for idx, (src, cw_ref, dst) in enumerate(((q_ref, wq_ref, qs_ref), (k_ref, wk_ref, ks_ref),
                                              (v_ref, wv_ref, vs_ref))):
        xp_ref[idx, halo:halo + tb, :] = src[...].astype(_F32)
        w = cw_ref[...]
        y = xp_ref[idx, halo - 3:halo - 3 + tb, :] * w[0:1, :]
        for j in range(1, _CONV_K):
            y = y + xp_ref[idx, halo - 3 + j:halo - 3 + j + tb, :] * w[j:j + 1, :]
        y = y * _sigmoid(y)
        for g in range(gh):
            yh = y[:, g * _HEAD_DIM:(g + 1) * _HEAD_DIM]
            if idx < 2:
                yh = yh * lax.rsqrt(jnp.sum(yh * yh, axis=1, keepdims=True) + _NORM_EPS)
            if idx == 0:
                yh = yh * (_HEAD_DIM ** -0.5)
            for ci in range(nc):
                dst[ci * gh + g] = yh[ci * c:(ci + 1) * c, :]

    ab = ab_ref[0]
    g_all = -jnp.exp(alog_ref[0]) * _softplus(ab + dtb_ref[0])
    beta_all = _sigmoid(ab)
    tri = tri_ref[...]
    pick = (lax.broadcasted_iota(jnp.int32, (8, _LANES), 0)
            == lax.broadcasted_iota(jnp.int32, (8, _LANES), 1)).astype(_BF16)
    for ci in range(nc):
        rows = slice(ci * c, (ci + 1) * c)
        gcum = _dot_exact_lhs(tri, g_all[rows, :])
        gcum_t = _dot_exact_lhs(pick, gcum, nt=True)
        for g in range(gh):
            gcol = gcum[:, g:g + 1]
            dm_ref[ci * gh + g] = gcol - gcum_t[g:g + 1, :]
            gc_ref[ci * gh + g] = jnp.broadcast_to(gcol, (c, _LANES))
            bt_ref[ci * gh + g] = jnp.broadcast_to(beta_all[rows, gh + g:gh + g + 1], (c, _LANES))

    ri = lax.broadcasted_iota(jnp.int32, (c, c), 0)
    cj = lax.broadcasted_iota(jnp.int32, (c, c), 1)
    incl, strict = ri >= cj, ri > cj
    sub_shift = int(math.log2(_GDN_SUB))
    blockdiag = lax.shift_right_logical(ri, sub_shift) == lax.shift_right_logical(cj, sub_shift)
    eye = (ri == cj).astype(_F32)

    gc = gc_ref[...]
    beta = bt_ref[...]
    decay = jnp.where(incl, jnp.exp(jnp.where(incl, dm_ref[...], 0.0)), 0.0)
    g_last = gc[:, c - 1:c, :]
    eg = jnp.exp(gc)
    dl_ref[...] = jnp.exp(g_last)
    q, k, v = qs_ref[...], ks_ref[...], vs_ref[...]
    k16 = k.astype(_BF16)
    kb = k * beta
    lm = jnp.where(strict, _bdot_nt(kb.astype(_BF16), k16) * decay, 0.0)
    tinv = _unit_lower_inverse(lm, eye, blockdiag)
    u_ref[...] = _bdot16(tinv, v * beta)
    w_ref[...] = _bdot16(tinv, kb * eg).astype(_BF16)
    in_ref[...] = (_bdot_nt(q.astype(_BF16), k16) * decay).astype(_BF16)
    qd_ref[...] = (q * eg).astype(_BF16)
    kd_ref[...] = (k * jnp.exp(g_last - gc)).astype(_BF16)

    nw = nw_ref[...]

    def chunk(cidx, _):
        sl = pl.ds(cidx * gh, gh)
        st = state_ref[...]
        st16 = st.astype(_BF16)
        v_new = u_ref[sl] - _bdot(w_ref[sl], st16)
        vn16 = v_new.astype(_BF16)
        o = _bdot(qd_ref[sl], st16) + _bdot(in_ref[sl], vn16)
        state_ref[...] = st * dl_ref[sl] + _bdot_tn(kd_ref[sl], vn16)
        o = o * lax.rsqrt(jnp.mean(o * o, axis=2, keepdims=True) + _NORM_EPS) * nw
        rows = pl.ds(pl.multiple_of(cidx * c, c), c)
        for g in range(gh):
            lanes = slice(g * _HEAD_DIM, (g + 1) * _HEAD_DIM)
            zf = z_ref[rows, lanes].astype(_F32)
            o_ref[rows, lanes] = (o[g] * (zf * _sigmoid(zf))).astype(o_ref.dtype)
        return 0

    lax.fori_loop(0, nc, chunk, 0)


def _gated_deltanet(proj, ab, conv_w, a_log, dt_bias, norm_w, bsz, seq, heads, tb, gh):
    t = bsz * seq
    w = heads * _HEAD_DIM
    tb = min(tb, seq)
    gh = min(gh, heads)
    assert gh <= 8, "one sublane tile holds the transposed per-head cumulative gates"
    nb = seq // tb
    ng = heads // gh
    c = _GDN_CHUNK
    nbatch = (tb // c) * gh
    ids = lax.broadcasted_iota(jnp.int32, (c, c), 0)
    tri = (ids >= ids.T).astype(_BF16)

    def group(x):
        return jnp.moveaxis(x.reshape(x.shape[0], ng, gh), 1, 0)

    lane_pad = ((0, 0), (0, 0), (0, _LANES - 2 * gh))
    ab_g = jnp.pad(jnp.concatenate([group(ab[:, :heads]), group(ab[:, heads:2 * heads])], axis=2), lane_pad)
    zeros = jnp.zeros((1, heads), _F32)
    alog_g = jnp.pad(jnp.concatenate([group(a_log.astype(_F32)[None]), group(zeros)], axis=2), lane_pad)
    dtb_g = jnp.pad(jnp.concatenate([group(dt_bias.astype(_F32)[None]), group(zeros)], axis=2), lane_pad)
    blk = (tb, gh * _HEAD_DIM)

    def col(base):
        return pl.BlockSpec(blk, lambda b, g, i: (b * nb + i, base * ng + g))

    def wcol(base):
        return pl.BlockSpec((_CONV_K, gh * _HEAD_DIM), lambda b, g, i: (0, base * ng + g))

    vec = pl.BlockSpec((1, 1, _LANES), lambda b, g, i: (g, 0, 0))
    batch_f32 = pltpu.VMEM((nbatch, c, _HEAD_DIM), _F32)
    batch_b16 = pltpu.VMEM((nbatch, c, _HEAD_DIM), _BF16)
    return pl.pallas_call(
        functools.partial(_gdn_kernel, tb=tb, gh=gh),
        out_shape=jax.ShapeDtypeStruct((t, w), _BF16),
        grid=(bsz, ng, nb),
        in_specs=[pl.BlockSpec((1, tb, _LANES), lambda b, g, i: (g, b * nb + i, 0)),
                  col(3), col(4), col(5), col(6), wcol(0), wcol(1), wcol(2), vec, vec,
                  pl.BlockSpec((1, _HEAD_DIM), lambda b, g, i: (0, 0)),
                  pl.BlockSpec((c, c), lambda b, g, i: (0, 0))],
        out_specs=pl.BlockSpec(blk, lambda b, g, i: (b * nb + i, g)),
        scratch_shapes=[pltpu.VMEM((3, tb + 8, gh * _HEAD_DIM), _F32),
                        batch_f32, batch_f32, batch_f32,
                        pltpu.VMEM((nbatch, c, c), _F32), batch_f32, batch_f32,
                        batch_f32, batch_b16, batch_b16, batch_b16,
                        pltpu.VMEM((nbatch, c, c), _BF16),
                        pltpu.VMEM((nbatch, 1, _LANES), _F32),
                        pltpu.VMEM((gh, _HEAD_DIM, _HEAD_DIM), _F32)],
        compiler_params=_params(("parallel", "parallel", "arbitrary")),
        name="gated_deltanet",
    )(ab_g, proj, proj, proj, proj, conv_w, conv_w, conv_w, alog_g, dtb_g,
      norm_w.reshape(1, _HEAD_DIM).astype(_F32), tri)


def _router_kernel(x_ref, w_ref, b_ref, tri_ref, route_ref, counts_ref, carry_ref, *, tm):
    i = pl.program_id(0)

    @pl.when(i == 0)
    def _():
        carry_ref[...] = jnp.zeros_like(carry_ref)

    logits = _dot_x3(x_ref[...], w_ref[...]) + b_ref[...]
    lane = lax.broadcasted_iota(jnp.int32, (tm, _LANES), 1)
    big = jnp.int32(_LANES)

    def top(vals):
        m = jnp.max(vals, axis=1, keepdims=True)
        idx = jnp.min(jnp.where(vals == m, lane, big), axis=1, keepdims=True)
        return m, idx

    gl = jnp.where(lane < _N_GROUPS, logits, _NEG)
    gm, gidx = top(gl)
    g_w = 1.0 / jnp.sum(jnp.exp(gl - gm), axis=1, keepdims=True)
    lane_group = lax.shift_right_arithmetic(lane - _N_GROUPS, int(math.log2(_EXPERTS_PER_GROUP)))
    el = jnp.where(lane_group == gidx, logits, _NEG)
    m1, i1 = top(el)
    m2, i2 = top(jnp.where(lane == i1, _NEG, el))
    r = jnp.exp(m2 - m1)
    w1 = g_w / (1.0 + r)
    w2 = g_w * r / (1.0 + r)
    e1, e2 = i1 - _N_GROUPS, i2 - _N_GROUPS
    oh1 = (lane == e1).astype(_F32)
    oh2 = (lane == e2).astype(_F32)
    both = oh1 + oh2
    before = _dot(tri_ref[...], both.astype(_BF16)) + carry_ref[0:1, :]
    rank1 = jnp.sum(oh1 * before, axis=1, keepdims=True)
    rank2 = jnp.sum(oh2 * before, axis=1, keepdims=True)
    out = jnp.zeros((tm, _LANES), _F32)
    for pos, val in enumerate((e1.astype(_F32), e2.astype(_F32), w1, w2, rank1, rank2)):
        out = jnp.where(lane == pos, val, out)
    route_ref[...] = out
    total = carry_ref[0:1, :] + jnp.sum(both, axis=0, keepdims=True)
    carry_ref[...] = jnp.broadcast_to(total, carry_ref.shape)
    counts_ref[...] = carry_ref[...]


def _router(x32, w_r, b_r, tm):
    t, d = x32.shape
    tm = min(tm, t)
    ids = lax.broadcasted_iota(jnp.int32, (tm, tm), 0)
    tri = (ids > ids.T).astype(_BF16)
    return pl.pallas_call(
        functools.partial(_router_kernel, tm=tm),
        out_shape=(jax.ShapeDtypeStruct((t, _LANES), _F32), jax.ShapeDtypeStruct((8, _LANES), _F32)),
        grid=(t // tm,),
        in_specs=[pl.BlockSpec((tm, d), lambda i: (i, 0)),
                  pl.BlockSpec((d, _LANES), lambda i: (0, 0)),
                  pl.BlockSpec((1, _LANES), lambda i: (0, 0)),
                  pl.BlockSpec((tm, tm), lambda i: (0, 0))],
        out_specs=(pl.BlockSpec((tm, _LANES), lambda i: (i, 0)),
                   pl.BlockSpec((8, _LANES), lambda i: (0, 0))),
        scratch_shapes=[pltpu.VMEM((8, _LANES), _F32)],
        compiler_params=_params(("arbitrary",)),
        name="moe_router",
    )(x32, w_r, b_r, tri)


_GATHER_UNROLL = 8


def _gather_rows(src_hbm, row_of, dst_ref, sem, n):
    def issue(r, _):
        pltpu.make_async_copy(src_hbm.at[pl.ds(row_of(r), 1)], dst_ref.at[pl.ds(r, 1)], sem).start()
        return 0

    lax.fori_loop(0, n, issue, 0, unroll=_GATHER_UNROLL)


def _wait_rows(src_hbm, dst_ref, sem, n):
    pltpu.make_async_copy(src_hbm.at[pl.ds(0, n)], dst_ref, sem).wait()


def _expert_kernel(te_ref, nact_ref, tok_ref, x_hbm, wg_ref, wu_ref, wd_ref, o_ref, buf_ref, sems, *, tm):
    i = pl.program_id(0)
    nact = nact_ref[0]
    slot = lax.rem(i, 2)

    def start(tile, s):
        base = tile * tm
        _gather_rows(x_hbm, lambda r: tok_ref[base + r], buf_ref.at[s], sems.at[s], tm)

    @pl.when(i == 0)
    def _():
        start(0, 0)

    @pl.when(i + 1 < nact)
    def _():
        start(i + 1, 1 - slot)

    @pl.when(i < nact)
    def _():
        _wait_rows(x_hbm, buf_ref.at[slot], sems.at[slot], tm)
        x = buf_ref[slot].astype(_BF16)
        hg = _dot(x, wg_ref[0])
        hu = _dot(x, wu_ref[0])
        h = (hg * _sigmoid(hg)) * hu
        o_ref[...] = _dot(h.astype(_BF16), wd_ref[0])

    @pl.when(i >= nact)
    def _():
        o_ref[...] = jnp.zeros_like(o_ref)


def _expert_ffn(x32, tok_of_slot, w_gate, w_up, w_down, layer, tile_expert, nact, n_tiles, tm):
    t, d = x32.shape
    f = w_gate.shape[3]
    return pl.pallas_call(
        functools.partial(_expert_kernel, tm=tm),
        out_shape=jax.ShapeDtypeStruct((n_tiles * tm, d), _F32),
        grid_spec=pltpu.PrefetchScalarGridSpec(
            num_scalar_prefetch=3,
            grid=(n_tiles,),
            in_specs=[pl.BlockSpec(memory_space=pl.ANY),
                      pl.BlockSpec((None, 1, d, f), lambda i, te, na, tok: (layer, te[i], 0, 0)),
                      pl.BlockSpec((None, 1, d, f), lambda i, te, na, tok: (layer, te[i], 0, 0)),
                      pl.BlockSpec((None, 1, f, d), lambda i, te, na, tok: (layer, te[i], 0, 0))],
            out_specs=pl.BlockSpec((tm, d), lambda i, te, na, tok: (i, 0)),
            scratch_shapes=[pltpu.VMEM((2, tm, d), _F32), pltpu.SemaphoreType.DMA((2,))]),
        compiler_params=_params(("arbitrary",)),
        name="moe_expert_ffn",
    )(tile_expert, nact, tok_of_slot, x32, w_gate, w_up, w_down)


def _combine_kernel(pos_ref, ys_hbm, route_ref, x_ref, g_ref, b_ref, o32_ref, o16_ref, buf_ref, sems,
                    *, tc, alpha):
    i = pl.program_id(0)
    slot = lax.rem(i, 2)

    def start(step, s):
        base = step * tc
        for k in range(2):
            _gather_rows(ys_hbm, lambda r, k=k: pos_ref[2 * (base + r) + k], buf_ref.at[s, k],
                         sems.at[s, k], tc)

    @pl.when(i == 0)
    def _():
        start(0, 0)

    @pl.when(i + 1 < pl.num_programs(0))
    def _():
        start(i + 1, 1 - slot)

    for k in range(2):
        _wait_rows(ys_hbm, buf_ref.at[slot, k], sems.at[slot, k], tc)
    route = route_ref[...]
    w1, w2 = route[:, 2:3], route[:, 3:4]
    y = alpha * x_ref[...] + (w1 * buf_ref[slot, 0] + w2 * buf_ref[slot, 1])
    out = _layer_norm_rows(y, g_ref[...], b_ref[...])
    o32_ref[...] = out
    o16_ref[...] = out.astype(_BF16)


def _combine_norm(ys, pos_flat, route, x32, g, b, alpha, tc):
    t, d = x32.shape
    tc = min(tc, t)
    row = lambda: pl.BlockSpec((tc, d), lambda i, pos: (i, 0))
    vec = lambda: pl.BlockSpec((1, d), lambda i, pos: (0, 0))
    return pl.pallas_call(
        functools.partial(_combine_kernel, tc=tc, alpha=alpha),
        out_shape=(jax.ShapeDtypeStruct((t, d), _F32), jax.ShapeDtypeStruct((t, d), _BF16)),
        grid_spec=pltpu.PrefetchScalarGridSpec(
            num_scalar_prefetch=1,
            grid=(t // tc,),
            in_specs=[pl.BlockSpec(memory_space=pl.ANY),
                      pl.BlockSpec((tc, _LANES), lambda i, pos: (i, 0)), row(), vec(), vec()],
            out_specs=(row(), row()),
            scratch_shapes=[pltpu.VMEM((2, 2, tc, d), _F32), pltpu.SemaphoreType.DMA((2, 2))]),
        compiler_params=_params(("arbitrary",)),
        name="moe_combine_norm",
    )(pos_flat, ys, route, x32, g.reshape(1, d), b.reshape(1, d))


def _moe_layer(x32, w_rg, b_rg, w_re, b_re, w_gate, w_up, w_down, layer, ln_g, ln_b, alpha, tm_r, tm_e, tc):
    t, d = x32.shape
    pad = _LANES - _N_GROUPS - _N_EXPERTS
    w_r = jnp.pad(jnp.concatenate([w_rg, w_re], axis=1).astype(_F32), ((0, 0), (0, pad)))
    b_r = jnp.pad(jnp.concatenate([b_rg, b_re]).astype(_F32), (0, pad)).reshape(1, _LANES)
    route, counts = _router(x32, w_r, b_r, tm_r)

    counts = counts[0, :_N_EXPERTS].astype(jnp.int32)
    padded = ((counts + tm_e - 1) // tm_e) * tm_e
    ends = jnp.cumsum(padded)
    starts = ends - padded
    eid = route[:, 0:2].astype(jnp.int32)
    pos = starts[eid] + route[:, 4:6].astype(jnp.int32)
    n_tiles = (2 * t) // tm_e + _N_EXPERTS
    nact = (ends[-1] // tm_e).astype(jnp.int32).reshape(1)
    tile_start = jnp.minimum(jnp.arange(n_tiles, dtype=jnp.int32), nact[0] - 1) * tm_e
    tile_expert = jnp.minimum(jnp.sum((ends[None, :] <= tile_start[:, None]).astype(jnp.int32), axis=1),
                              _N_EXPERTS - 1)
    tok = jnp.broadcast_to(jnp.arange(t, dtype=jnp.int32)[:, None], (t, 2))
    tok_of_slot = jnp.zeros((n_tiles * tm_e,), jnp.int32).at[pos.reshape(-1)].set(tok.reshape(-1))

    ys = _expert_ffn(x32, tok_of_slot, w_gate, w_up, w_down, layer, tile_expert, nact, n_tiles, tm_e)
    return _combine_norm(ys, pos.reshape(-1), route, x32, ln_g, ln_b, alpha, tc)


def kernel(x, w_in, conv_w, gdn_a_log, gdn_dt_bias, gdn_norm_w, diff_lambda_q1, diff_lambda_k1,
           diff_lambda_q2, diff_lambda_k2, diff_subln_w, w_branch_sba, w_branch_gdn, w_branch_diff,
           w_out, ln1_g, ln1_b, w_router_group, b_router_group, w_router_expert, b_router_expert,
           w_expert_gate, w_expert_up, w_expert_down, ln2_g, ln2_b):
    bsz, seq, d = x.shape
    depth = w_in.shape[0]
    t = bsz * seq
    heads = d // 256
    w = heads * _HEAD_DIM
    alpha = (2 * depth) ** 0.25
    big = t >= 8192
    tm_e = 256 if big else 64

    tables = _rope_tables(seq)
    ab0 = 3 * w + 4 * w
    ab1 = ab0 + 2 * heads

    w_t = jnp.swapaxes(w_in, 1, 2).astype(_BF16)
    wb_sba, wb_gdn, wb_diff = (wb.astype(_BF16) for wb in (w_branch_sba, w_branch_gdn, w_branch_diff))
    w_out16 = w_out.astype(_BF16)
    w_gate16, w_up16, w_down16 = (we.astype(_BF16) for we in (w_expert_gate, w_expert_up, w_expert_down))

    x32 = x.reshape(t, d)
    x16 = x32.astype(_BF16)
    for l in range(depth):
        proj_a = _matmul_nt(x16, w_t, l, 0, ab0, _BF16, 1024, 1024, "in_proj_a")
        proj_b = _matmul_nt(x16, w_t, l, ab1, w_t.shape[1] - ab1, _BF16, 1024, 1024, "in_proj_b")
        ab = _matmul_nt(x16, w_t, l, ab0, ab1 - ab0, _F32, 1024, _LANES, "in_proj_ab")

        y_sba = _stick_breaking(proj_a, bsz, seq, heads, 512, 256)
        y_gdn = _gated_deltanet(proj_a, ab, conv_w[l], gdn_a_log[l], gdn_dt_bias[l], gdn_norm_w[l],
                                bsz, seq, heads, 256, 8)
        q_r, k_r = _rope(proj_b, tables, bsz, seq, heads, 512)
        lam_vecs = jnp.stack([diff_lambda_q1[l], diff_lambda_k1[l], diff_lambda_q2[l],
                              diff_lambda_k2[l]]).astype(_F32)
        lam_init = 0.8 - 0.6 * math.exp(-0.3 * l)
        y_diff = _diff_attention(q_r, k_r, proj_b, lam_vecs, diff_subln_w[l].astype(_F32), lam_init,
                                 bsz, seq, heads, 512, 256)

        merged = _branch_merge(y_sba, y_gdn, y_diff, wb_sba, wb_gdn, wb_diff, l, proj_b, 3 * w, 512, 512)
        h = _outproj_residual(merged, w_out16, l, x32, alpha, 1024, 512)
        x32, x16 = _layer_norm(h, ln1_g[l], ln1_b[l], 256)

        x32, x16 = _moe_layer(x32, w_router_group[l], b_router_group[l], w_router_expert[l],
                              b_router_expert[l], w_gate16, w_up16, w_down16, l,
                              ln2_g[l], ln2_b[l], alpha, 512, tm_e, 128)
    return x32.reshape(bsz, seq, d)
```

```python
import functools
import math

import jax
import jax.numpy as jnp
from jax import lax
from jax.experimental import pallas as pl
from jax.experimental.pallas import tpu as pltpu

_F32 = jnp.float32
_BF16 = jnp.bfloat16

_LANES = 128
_VMEM_LIMIT = 56 * 1024 * 1024
_HEAD_DIM = 128
_DIFF_DH = 64
_GDN_CHUNK = 64
_GDN_SUB = 16
_CONV_K = 4
_ROPE_THETA = 10000.0
_N_GROUPS = 4
_EXPERTS_PER_GROUP = 8
_N_EXPERTS = _N_GROUPS * _EXPERTS_PER_GROUP
_LN_EPS = 1e-5
_NORM_EPS = 1e-6
_NEG = -1e30
_LOG2E = 1.4426950408889634


def _params(sem):
    return pltpu.CompilerParams(dimension_semantics=sem, vmem_limit_bytes=_VMEM_LIMIT)


def _nt_dot(a, b):
    return lax.dot_general(a, b, (((1,), (1,)), ((), ())), preferred_element_type=_F32)


def _dot(a, b):
    return jnp.dot(a, b, preferred_element_type=_F32)


def _split2(a):
    hi = a.astype(_BF16)
    lo = (a - hi.astype(_F32)).astype(_BF16)
    return hi, lo


def _dot_x3(a, b, nt=False):
    f = _nt_dot if nt else _dot
    ah, al = _split2(a)
    bh, bl = _split2(b)
    return f(ah, bh) + (f(ah, bl) + f(al, bh))


def _dot_exact_lhs(a_bf16, b, nt=False):
    f = _nt_dot if nt else _dot
    b1 = b.astype(_BF16)
    r1 = b - b1.astype(_F32)
    b2 = r1.astype(_BF16)
    b3 = (r1 - b2.astype(_F32)).astype(_BF16)
    return f(a_bf16, b1) + (f(a_bf16, b2) + f(a_bf16, b3))


def _sigmoid(x):
    return 1.0 / (1.0 + jnp.exp(-x))


def _softplus(x):
    return jnp.maximum(x, 0.0) + jnp.log(1.0 + jnp.exp(-jnp.abs(x)))


def _mm_kernel(x_ref, w_ref, o_ref):
    o_ref[...] = _nt_dot(x_ref[...], w_ref[0]).astype(o_ref.dtype)


def _matmul_nt(x, w, layer, row0, n, out_dtype, tm, tn, name):
    m, k = x.shape
    tm, tn = min(tm, m), min(tn, n)
    while n % tn:
        tn //= 2
    return pl.pallas_call(
        _mm_kernel,
        out_shape=jax.ShapeDtypeStruct((m, n), out_dtype),
        grid=(m // tm, n // tn),
        in_specs=[pl.BlockSpec((tm, k), lambda i, j: (i, 0)),
                  pl.BlockSpec((pl.Element(1), pl.Element(tn), pl.Element(k)),
                               lambda i, j: (layer, pl.multiple_of(row0 + j * tn, math.gcd(row0, tn)), 0))],
        out_specs=pl.BlockSpec((tm, tn), lambda i, j: (i, j)),
        compiler_params=_params(("parallel", "arbitrary")),
        name=name,
    )(x, w)


def _merge_kernel(ys_ref, yg_ref, yd_ref, ws_ref, wg_ref, wd_ref, gs_ref, gg_ref, gd_ref, o_ref):
    acc = _sigmoid(gs_ref[...].astype(_F32)) * _dot(ys_ref[...], ws_ref[...])
    acc += _sigmoid(gg_ref[...].astype(_F32)) * _dot(yg_ref[...], wg_ref[...])
    acc += _sigmoid(gd_ref[...].astype(_F32)) * _dot(yd_ref[...], wd_ref[...])
    o_ref[...] = acc.astype(o_ref.dtype)


def _branch_merge(y_sba, y_gdn, y_diff, wb_sba, wb_gdn, wb_diff, layer, proj, gate_col0, tm, tn):
    t, w = y_sba.shape
    d = wb_sba.shape[2]
    tm, tn = min(tm, t), min(tn, d)
    g0 = gate_col0 // tn
    nd = d // tn
    y_spec = pl.BlockSpec((tm, w), lambda i, j: (i, 0))
    w_spec = pl.BlockSpec((None, w, tn), lambda i, j: (layer, 0, j))

    def gate_spec(b):
        return pl.BlockSpec((tm, tn), lambda i, j: (i, g0 + b * nd + j))

    return pl.pallas_call(
        _merge_kernel,
        out_shape=jax.ShapeDtypeStruct((t, d), _BF16),
        grid=(t // tm, nd),
        in_specs=[y_spec, y_spec, y_spec, w_spec, w_spec, w_spec,
                  gate_spec(0), gate_spec(1), gate_spec(2)],
        out_specs=pl.BlockSpec((tm, tn), lambda i, j: (i, j)),
        compiler_params=_params(("parallel", "arbitrary")),
        name="branch_merge",
    )(y_sba, y_gdn, y_diff, wb_sba, wb_gdn, wb_diff, proj, proj, proj)


def _outproj_kernel(m_ref, w_ref, x_ref, o_ref, *, alpha):
    o_ref[...] = alpha * x_ref[...] + _dot(m_ref[...], w_ref[...])


def _outproj_residual(merged, w_out, layer, x, alpha, tm, tn):
    t, d = merged.shape
    tm, tn = min(tm, t), min(tn, d)
    return pl.pallas_call(
        functools.partial(_outproj_kernel, alpha=alpha),
        out_shape=jax.ShapeDtypeStruct((t, d), _F32),
        grid=(t // tm, d // tn),
        in_specs=[pl.BlockSpec((tm, d), lambda i, j: (i, 0)),
                  pl.BlockSpec((None, d, tn), lambda i, j: (layer, 0, j)),
                  pl.BlockSpec((tm, tn), lambda i, j: (i, j))],
        out_specs=pl.BlockSpec((tm, tn), lambda i, j: (i, j)),
        compiler_params=_params(("parallel", "arbitrary")),
        name="outproj_residual",
    )(merged, w_out, x)


def _layer_norm_rows(y, g, b):
    mu = jnp.mean(y, axis=-1, keepdims=True)
    yc = y - mu
    var = jnp.mean(yc * yc, axis=-1, keepdims=True)
    return yc * lax.rsqrt(var + _LN_EPS) * g + b


def _ln_kernel(y_ref, g_ref, b_ref, o32_ref, o16_ref):
    out = _layer_norm_rows(y_ref[...], g_ref[...], b_ref[...])
    o32_ref[...] = out
    o16_ref[...] = out.astype(_BF16)


def _layer_norm(y, g, b, tr):
    t, d = y.shape
    tr = min(tr, t)
    row = pl.BlockSpec((tr, d), lambda i: (i, 0))
    vec = pl.BlockSpec((1, d), lambda i: (0, 0))
    return pl.pallas_call(
        _ln_kernel,
        out_shape=(jax.ShapeDtypeStruct((t, d), _F32), jax.ShapeDtypeStruct((t, d), _BF16)),
        grid=(t // tr,),
        in_specs=[row, vec, vec],
        out_specs=(row, row),
        compiler_params=_params(("parallel",)),
        name="layer_norm",
    )(y, g.reshape(1, d), b.reshape(1, d))


def _neg_abs(x):
    bits = lax.bitcast_convert_type(x, jnp.uint32) | jnp.uint32(0x80000000)
    return lax.bitcast_convert_type(bits, _F32)


_SWEEP_UNROLLS = (8, 4, 2)


def _causal_sweep(run, n_blocks, carry):
    done = 0
    for un in _SWEEP_UNROLLS:
        trips = (n_blocks - done) // un
        first = n_blocks - 1 - done
        carry = lax.fori_loop(0, trips, lambda i, c, un=un, first=first: run(first - un * i, c, un, False),
                              carry)
        done = done + trips * un
    return carry


def _sba_kernel(q_ref, k_ref, v_ref, u_ref, o_ref, acc_ref, w_ref, *, tq, tk, scale):
    qi = pl.program_id(2)
    q = (q_ref[...].astype(_F32) * (scale * _LOG2E)).astype(_BF16)
    u = u_ref[...]
    n_diag = tq // tk
    acc_ref[...] = jnp.zeros_like(acc_ref)

    def put_scores(slot, kb, r0):
        ks = pl.multiple_of(jnp.maximum(kb, 0) * tk, tk)
        w_ref[slot, r0:, :] = _nt_dot(q[r0:, :], k_ref[pl.ds(ks, tk), :])

    def consume(slot, kb, carry, masked, r0):
        v = v_ref[pl.ds(pl.multiple_of(kb * tk, tk), tk), :]
        w = w_ref[slot, r0:, :]
        sp = jnp.maximum(w, 0.0) + jnp.log(1.0 + jnp.exp2(_neg_abs(w))) * _LOG2E
        if masked:
            rows = qi * tq + r0 + lax.broadcasted_iota(jnp.int32, (tq - r0, tk), 0)
            cols = kb * tk + lax.broadcasted_iota(jnp.int32, (tq - r0, tk), 1)
            strict = cols < rows
            sp = jnp.where(strict, sp, 0.0)
        later = _dot(sp.astype(_BF16), u)
        att = jnp.exp2(((w - sp) - later) - carry[r0:, :])
        if masked:
            att = jnp.where(strict, att, 0.0)
        acc_ref[r0:, :] += _dot(att.astype(_BF16), v)
        new = carry[r0:, :] + jnp.sum(sp, axis=1, keepdims=True)
        return new if r0 == 0 else jnp.concatenate([carry[:r0, :], new], axis=0)

    def run(kb0, carry, nblk, masked):
        for j in range(nblk):
            r0 = (nblk - 1 - j) * tk if masked else 0
            put_scores((j + 1) % 2, kb0 - j - 1, max(r0 - tk, 0))
            carry = consume(j % 2, kb0 - j, carry, masked, r0)
        return carry

    top = (qi + 1) * n_diag - 1
    put_scores(0, top, (n_diag - 1) * tk)
    carry = run(top, jnp.zeros((tq, 1), _F32), n_diag, True)
    n_full = qi * n_diag
    _causal_sweep(run, n_full, carry)
    o_ref[...] = acc_ref[...].astype(o_ref.dtype)


def _stick_breaking(proj, bsz, seq, heads, tq, tk):
    t = bsz * seq
    tq, tk = min(tq, seq), min(tk, seq)
    tk = min(tk, tq)
    nq = seq // tq
    assert (tq // tk) % 2 == 0, "the two score slots alternate per key block"
    ids = lax.broadcasted_iota(jnp.int32, (tk, tk), 0)
    u = (ids > ids.T).astype(_BF16)
    return pl.pallas_call(
        functools.partial(_sba_kernel, tq=tq, tk=tk, scale=_HEAD_DIM ** -0.5),
        out_shape=jax.ShapeDtypeStruct((t, heads * _HEAD_DIM), _BF16),
        grid=(bsz, heads, nq),
        in_specs=[pl.BlockSpec((tq, _HEAD_DIM), lambda b, h, i: (b * nq + i, h)),
                  pl.BlockSpec((seq, _HEAD_DIM), lambda b, h, i: (b, heads + h)),
                  pl.BlockSpec((seq, _HEAD_DIM), lambda b, h, i: (b, 2 * heads + h)),
                  pl.BlockSpec((tk, tk), lambda b, h, i: (0, 0))],
        out_specs=pl.BlockSpec((tq, _HEAD_DIM), lambda b, h, i: (b * nq + i, h)),
        scratch_shapes=[pltpu.VMEM((tq, _HEAD_DIM), _F32), pltpu.VMEM((2, tq, tk), _F32)],
        compiler_params=_params(("parallel", "parallel", "arbitrary")),
        name="stick_breaking_attention",
    )(proj, proj, proj, u)


def _rope_kernel(q_ref, k_ref, cos_ref, sa_ref, sb_ref, qo_ref, ko_ref, *, heads, scale):
    cos, sa, sb = cos_ref[...], sa_ref[...], sb_ref[...]
    for h in range(heads):
        sl = slice(h * _HEAD_DIM, (h + 1) * _HEAD_DIM)
        for src, dst, s in ((q_ref, qo_ref, scale), (k_ref, ko_ref, 1.0)):
            x = src[:, sl].astype(_F32)
            r = (x * cos + pltpu.roll(x, _HEAD_DIM - _DIFF_DH // 2, axis=1) * sa
                 + pltpu.roll(x, _DIFF_DH // 2, axis=1) * sb)
            dst[:, sl] = (r * s).astype(dst.dtype)


def _rope_tables(seq):
    half = _DIFF_DH // 2
    pos = jnp.arange(seq, dtype=_F32)
    inv_freq = _ROPE_THETA ** (-jnp.arange(0, _DIFF_DH, 2, dtype=_F32) / _DIFF_DH)
    ang = pos[:, None] * inv_freq[None, :]
    cos, sin, zero = jnp.cos(ang), jnp.sin(ang), jnp.zeros_like(ang)
    cos_t = jnp.concatenate([cos] * 4, axis=-1)
    sa_t = jnp.concatenate([-sin, zero] * 2, axis=-1)
    sb_t = jnp.concatenate([zero, sin] * 2, axis=-1)
    return cos_t, sa_t, sb_t


def _rope(proj, tables, bsz, seq, heads, ts):
    t = bsz * seq
    w = heads * _HEAD_DIM
    ts = min(ts, seq)
    ns = seq // ts
    tab = pl.BlockSpec((ts, _HEAD_DIM), lambda i: (i % ns, 0))
    out = pl.BlockSpec((ts, w), lambda i: (i, 0))
    return pl.pallas_call(
        functools.partial(_rope_kernel, heads=heads, scale=_DIFF_DH ** -0.5 * _LOG2E),
        out_shape=(jax.ShapeDtypeStruct((t, w), _BF16), jax.ShapeDtypeStruct((t, w), _BF16)),
        grid=(t // ts,),
        in_specs=[pl.BlockSpec((ts, w), lambda i: (i, 0)),
                  pl.BlockSpec((ts, w), lambda i: (i, 1)), tab, tab, tab],
        out_specs=(out, out),
        compiler_params=_params(("parallel",)),
        name="diff_rope",
    )(proj, proj, *tables)


def _diff_kernel(lam_ref, sw_ref, q_ref, k_ref, v_ref, o_ref, acc_ref, s_ref, *, tq, tk, lam_init):
    qi = pl.program_id(2)
    q = q_ref[...]
    lane = lax.broadcasted_iota(jnp.int32, (tq, _HEAD_DIM), 1)
    zero = jnp.zeros_like(q)
    qs = (jnp.where(lane < _DIFF_DH, q, zero), jnp.where(lane >= _DIFF_DH, q, zero))
    n_diag = tq // tk
    acc_ref[...] = jnp.zeros_like(acc_ref)

    def put_scores(slot, kb, r0):
        ks = pl.multiple_of(jnp.maximum(kb, 0) * tk, tk)
        k = k_ref[pl.ds(ks, tk), :]
        for c in range(2):
            s_ref[slot, c, :, r0:] = _nt_dot(k, qs[c][r0:, :])

    def consume(slot, kb, carry, masked, r0):
        v = v_ref[pl.ds(pl.multiple_of(kb * tk, tk), tk), :]
        if masked:
            keys = kb * tk + lax.broadcasted_iota(jnp.int32, (tk, tq - r0), 0)
            qpos = qi * tq + r0 + lax.broadcasted_iota(jnp.int32, (tk, tq - r0), 1)
            causal = keys <= qpos
        out = []
        for c in range(2):
            m_prev, l_prev = carry[2 * c][:, r0:], carry[2 * c + 1][:, r0:]
            s = s_ref[slot, c, :, r0:]
            if masked:
                s = jnp.where(causal, s, _NEG)
            m_new = jnp.maximum(m_prev, jnp.max(s, axis=0, keepdims=True))
            p = jnp.exp2(s - m_new)
            alpha = jnp.exp2(m_prev - m_new)
            pv = lax.dot_general(v, p.astype(_BF16), (((0,), (0,)), ((), ())), preferred_element_type=_F32)
            acc_ref[c, :, r0:] = alpha * acc_ref[c, :, r0:] + pv
            new = [m_new, alpha * l_prev + jnp.sum(p, axis=0, keepdims=True)]
            if r0:
                new = [jnp.concatenate([carry[2 * c + i][:, :r0], new[i]], axis=1) for i in range(2)]
            out += new
        return tuple(out)

    def run(kb0, carry, nblk, masked):
        for j in range(nblk):
            r0 = (nblk - 1 - j) * tk if masked else 0
            put_scores((j + 1) % 2, kb0 - j - 1, max(r0 - tk, 0))
            carry = consume(j % 2, kb0 - j, carry, masked, r0)
        return carry

    neg = jnp.full((1, tq), _NEG, _F32)
    zero_row = jnp.zeros((1, tq), _F32)
    top = (qi + 1) * n_diag - 1
    put_scores(0, top, (n_diag - 1) * tk)
    carry = run(top, (neg, zero_row, neg, zero_row), n_diag, True)
    n_full = qi * n_diag
    carry = _causal_sweep(run, n_full, carry)

    lv = lam_ref[...]
    lam = (jnp.exp(jnp.sum(lv[0:1] * lv[1:2], axis=1, keepdims=True))
           - jnp.exp(jnp.sum(lv[2:3] * lv[3:4], axis=1, keepdims=True)) + lam_init)
    o = acc_ref[0] / carry[1] - lam * (acc_ref[1] / carry[3])
    o = o * lax.rsqrt(jnp.mean(o * o, axis=0, keepdims=True) + _LN_EPS)
    o_ref[...] = (o.T * sw_ref[...] * (1.0 - lam_init)).astype(o_ref.dtype)


def _diff_attention(q_r, k_r, proj, lam_vecs, subln_w, lam_init, bsz, seq, heads, tq, tk):
    t = bsz * seq
    tq, tk = min(tq, seq), min(tk, seq)
    tk = min(tk, tq)
    nq = seq // tq
    return pl.pallas_call(
        functools.partial(_diff_kernel, tq=tq, tk=tk, lam_init=lam_init),
        out_shape=jax.ShapeDtypeStruct((t, heads * _HEAD_DIM), _BF16),
        grid=(bsz, heads, nq),
        in_specs=[pl.BlockSpec((4, _DIFF_DH), lambda b, h, i: (0, 0)),
                  pl.BlockSpec((1, _HEAD_DIM), lambda b, h, i: (0, 0)),
                  pl.BlockSpec((tq, _HEAD_DIM), lambda b, h, i: (b * nq + i, h)),
                  pl.BlockSpec((seq, _HEAD_DIM), lambda b, h, i: (b, h)),
                  pl.BlockSpec((seq, _HEAD_DIM), lambda b, h, i: (b, 2 * heads + h))],
        out_specs=pl.BlockSpec((tq, _HEAD_DIM), lambda b, h, i: (b * nq + i, h)),
        scratch_shapes=[pltpu.VMEM((2, _HEAD_DIM, tq), _F32), pltpu.VMEM((2, 2, tk, tq), _F32)],
        compiler_params=_params(("parallel", "parallel", "arbitrary")),
        name="differential_attention",
    )(lam_vecs, subln_w.reshape(1, _HEAD_DIM), q_r, k_r, proj)


def _bdot(a, b):
    return lax.dot_general(a, b, (((2,), (1,)), ((0,), (0,))), preferred_element_type=_F32)


def _bdot_nt(a, b):
    return lax.dot_general(a, b, (((2,), (2,)), ((0,), (0,))), preferred_element_type=_F32)


def _bdot_tn(a, b):
    return lax.dot_general(a, b, (((1,), (1,)), ((0,), (0,))), preferred_element_type=_F32)


def _bdot16(a, b):
    return _bdot(a.astype(_BF16), b.astype(_BF16))


def _unit_lower_inverse(lm, eye, blockdiag):
    c = lm.shape[-1]
    md = jnp.where(blockdiag, -lm, 0.0)
    x = eye + md
    p = _bdot16(md, md)
    steps = int(math.log2(_GDN_SUB)) - 1
    for it in range(steps):
        x = x + _bdot16(x, p)
        if it + 1 < steps:
            p = _bdot16(p, p)
    n = _bdot16(x, jnp.where(blockdiag, 0.0, lm))
    y = eye - n
    pw = n
    for _ in range(int(math.log2(c // _GDN_SUB)) - 1):
        pw = _bdot16(pw, pw)
        y = y + _bdot16(y, pw)
    return _bdot16(y, x)


def _gdn_kernel(ab_ref, q_ref, k_ref, v_ref, z_ref, wq_ref, wk_ref, wv_ref, alog_ref, dtb_ref, nw_ref,
                tri_ref, o_ref, xp_ref, qs_ref, ks_ref, vs_ref, dm_ref, gc_ref, bt_ref,
                u_ref, w_ref, qd_ref, kd_ref, in_ref, dl_ref, state_ref, *, tb, gh):
    i = pl.program_id(2)
    c = _GDN_CHUNK
    nc = tb // c
    halo = 8

    @pl.when(i == 0)
    def _():
        xp_ref[:, 0:halo, :] = jnp.zeros((3, halo, gh * _HEAD_DIM), _F32)
        state_ref[...] = jnp.zeros_like(state_ref)

    @pl.when(i > 0)
    def _():
        xp_ref[:, 0:halo, :] = xp_ref[:, tb:tb + halo, :]

    for idx, (src, cw_ref, dst) in enumerate(((q_ref, wq_ref, qs_ref), (k_ref, wk_ref, ks_ref),
                                              (v_ref, wv_ref, vs_ref))):
        xp_ref[idx, halo:halo + tb, :] = src[...].astype(_F32)
        w = cw_ref[...]
        y = xp_ref[idx, halo - 3:halo - 3 + tb, :] * w[0:1, :]
        for j in range(1, _CONV_K):
            y = y + xp_ref[idx, halo - 3 + j:halo - 3 + j + tb, :] * w[j:j + 1, :]
        y = y * _sigmoid(y)
        for g in range(gh):
            yh = y[:, g * _HEAD_DIM:(g + 1) * _HEAD_DIM]
            if idx < 2:
                yh = yh * lax.rsqrt(jnp.sum(yh * yh, axis=1, keepdims=True) + _NORM_EPS)
            if idx == 0:
                yh = yh * (_HEAD_DIM ** -0.5)
            for ci in range(nc):
                dst[ci * gh + g] = yh[ci * c:(ci + 1) * c, :]

    ab = ab_ref[0]
    g_all = -jnp.exp(alog_ref[0]) * _softplus(ab + dtb_ref[0])
    beta_all = _sigmoid(ab)
    tri = tri_ref[...]
    pick = (lax.broadcasted_iota(jnp.int32, (8, _LANES), 0)
            == lax.broadcasted_iota(jnp.int32, (8, _LANES), 1)).astype(_BF16)
    for ci in range(nc):
        rows = slice(ci * c, (ci + 1) * c)
        gcum = _dot_exact_lhs(tri, g_all[rows, :])
        gcum_t = _dot_exact_lhs(pick, gcum, nt=True)
        for g in range(gh):
            gcol = gcum[:, g:g + 1]
            dm_ref[ci * gh + g] = gcol - gcum_t[g:g + 1, :]
            gc_ref[ci * gh + g] = jnp.broadcast_to(gcol, (c, _LANES))
            bt_ref[ci * gh + g] = jnp.broadcast_to(beta_all[rows, gh + g:gh + g + 1], (c, _LANES))

    ri = lax.broadcasted_iota(jnp.int32, (c, c), 0)
    cj = lax.broadcasted_iota(jnp.int32, (c, c), 1)
    incl, strict = ri >= cj, ri > cj
    sub_shift = int(math.log2(_GDN_SUB))
    blockdiag = lax.shift_right_logical(ri, sub_shift) == lax.shift_right_logical(cj, sub_shift)
    eye = (ri == cj).astype(_F32)

    gc = gc_ref[...]
    beta = bt_ref[...]
    decay = jnp.where(incl, jnp.exp(jnp.where(incl, dm_ref[...], 0.0)), 0.0)
    g_last = gc[:, c - 1:c, :]
    eg = jnp.exp(gc)
    dl_ref[...] = jnp.exp(g_last)
    q, k, v = qs_ref[...], ks_ref[...], vs_ref[...]
    k16 = k.astype(_BF16)
    kb = k * beta
    lm = jnp.where(strict, _bdot_nt(kb.astype(_BF16), k16) * decay, 0.0)
    tinv = _unit_lower_inverse(lm, eye, blockdiag)
    u_ref[...] = _bdot16(tinv, v * beta)
    w_ref[...] = _bdot16(tinv, kb * eg).astype(_BF16)
    in_ref[...] = (_bdot_nt(q.astype(_BF16), k16) * decay).astype(_BF16)
    qd_ref[...] = (q * eg).astype(_BF16)
    kd_ref[...] = (k * jnp.exp(g_last - gc)).astype(_BF16)

    nw = nw_ref[...]

    def chunk(cidx, _):
        sl = pl.ds(cidx * gh, gh)
        st = state_ref[...]
        st16 = st.astype(_BF16)
        v_new = u_ref[sl] - _bdot(w_ref[sl], st16)
        vn16 = v_new.astype(_BF16)
        o = _bdot(qd_ref[sl], st16) + _bdot(in_ref[sl], vn16)
        state_ref[...] = st * dl_ref[sl] + _bdot_tn(kd_ref[sl], vn16)
        o = o * lax.rsqrt(jnp.mean(o * o, axis=2, keepdims=True) + _NORM_EPS) * nw
        rows = pl.ds(pl.multiple_of(cidx * c, c), c)
        for g in range(gh):
            lanes = slice(g * _HEAD_DIM, (g + 1) * _HEAD_DIM)
            zf = z_ref[rows, lanes].astype(_F32)
            o_ref[rows, lanes] = (o[g] * (zf * _sigmoid(zf))).astype(o_ref.dtype)
        return 0

    lax.fori_loop(0, nc, chunk, 0)


def _gated_deltanet(proj, ab, conv_w, a_log, dt_bias, norm_w, bsz, seq, heads, tb, gh):
    t = bsz * seq
    w = heads * _HEAD_DIM
    tb = min(tb, seq)
    gh = min(gh, heads)
    assert gh <= 8, "one sublane tile holds the transposed per-head cumulative gates"
    nb = seq // tb
    ng = heads // gh
    c = _GDN_CHUNK
    nbatch = (tb // c) * gh
    ids = lax.broadcasted_iota(jnp.int32, (c, c), 0)
    tri = (ids >= ids.T).astype(_BF16)

    def group(x):
        return jnp.moveaxis(x.reshape(x.shape[0], ng, gh), 1, 0)

    lane_pad = ((0, 0), (0, 0), (0, _LANES - 2 * gh))
    ab_g = jnp.pad(jnp.concatenate([group(ab[:, :heads]), group(ab[:, heads:2 * heads])], axis=2), lane_pad)
    zeros = jnp.zeros((1, heads), _F32)
    alog_g = jnp.pad(jnp.concatenate([group(a_log.astype(_F32)[None]), group(zeros)], axis=2), lane_pad)
    dtb_g = jnp.pad(jnp.concatenate([group(dt_bias.astype(_F32)[None]), group(zeros)], axis=2), lane_pad)
    blk = (tb, gh * _HEAD_DIM)

    def col(base):
        return pl.BlockSpec(blk, lambda b, g, i: (b * nb + i, base * ng + g))

    def wcol(base):
        return pl.BlockSpec((_CONV_K, gh * _HEAD_DIM), lambda b, g, i: (0, base * ng + g))

    vec = pl.BlockSpec((1, 1, _LANES), lambda b, g, i: (g, 0, 0))
    batch_f32 = pltpu.VMEM((nbatch, c, _HEAD_DIM), _F32)
    batch_b16 = pltpu.VMEM((nbatch, c, _HEAD_DIM), _BF16)
    return pl.pallas_call(
        functools.partial(_gdn_kernel, tb=tb, gh=gh),
        out_shape=jax.ShapeDtypeStruct((t, w), _BF16),
        grid=(bsz, ng, nb),
        in_specs=[pl.BlockSpec((1, tb, _LANES), lambda b, g, i: (g, b * nb + i, 0)),
                  col(3), col(4), col(5), col(6), wcol(0), wcol(1), wcol(2), vec, vec,
                  pl.BlockSpec((1, _HEAD_DIM), lambda b, g, i: (0, 0)),
                  pl.BlockSpec((c, c), lambda b, g, i: (0, 0))],
        out_specs=pl.BlockSpec(blk, lambda b, g, i: (b * nb + i, g)),
        scratch_shapes=[pltpu.VMEM((3, tb + 8, gh * _HEAD_DIM), _F32),
                        batch_f32, batch_f32, batch_f32,
                        pltpu.VMEM((nbatch, c, c), _F32), batch_f32, batch_f32,
                        batch_f32, batch_b16, batch_b16, batch_b16,
                        pltpu.VMEM((nbatch, c, c), _BF16),
                        pltpu.VMEM((nbatch, 1, _LANES), _F32),
                        pltpu.VMEM((gh, _HEAD_DIM, _HEAD_DIM), _F32)],
        compiler_params=_params(("parallel", "parallel", "arbitrary")),
        name="gated_deltanet",
    )(ab_g, proj, proj, proj, proj, conv_w, conv_w, conv_w, alog_g, dtb_g,
      norm_w.reshape(1, _HEAD_DIM).astype(_F32), tri)


def _router_kernel(x_ref, w_ref, b_ref, tri_ref, route_ref, counts_ref, carry_ref, *, tm):
    i = pl.program_id(0)

    @pl.when(i == 0)
    def _():
        carry_ref[...] = jnp.zeros_like(carry_ref)

    logits = _dot_x3(x_ref[...], w_ref[...]) + b_ref[...]
    lane = lax.broadcasted_iota(jnp.int32, (tm, _LANES), 1)
    big = jnp.int32(_LANES)

    def top(vals):
        m = jnp.max(vals, axis=1, keepdims=True)
        idx = jnp.min(jnp.where(vals == m, lane, big), axis=1, keepdims=True)
        return m, idx

    gl = jnp.where(lane < _N_GROUPS, logits, _NEG)
    gm, gidx = top(gl)
    g_w = 1.0 / jnp.sum(jnp.exp(gl - gm), axis=1, keepdims=True)
    lane_group = lax.shift_right_arithmetic(lane - _N_GROUPS, int(math.log2(_EXPERTS_PER_GROUP)))
    el = jnp.where(lane_group == gidx, logits, _NEG)
    m1, i1 = top(el)
    m2, i2 = top(jnp.where(lane == i1, _NEG, el))
    r = jnp.exp(m2 - m1)
    w1 = g_w / (1.0 + r)
    w2 = g_w * r / (1.0 + r)
    e1, e2 = i1 - _N_GROUPS, i2 - _N_GROUPS
    oh1 = (lane == e1).astype(_F32)
    oh2 = (lane == e2).astype(_F32)
    both = oh1 + oh2
    before = _dot(tri_ref[...], both.astype(_BF16)) + carry_ref[0:1, :]
    rank1 = jnp.sum(oh1 * before, axis=1, keepdims=True)
    rank2 = jnp.sum(oh2 * before, axis=1, keepdims=True)
    out = jnp.zeros((tm, _LANES), _F32)
    for pos, val in enumerate((e1.astype(_F32), e2.astype(_F32), w1, w2, rank1, rank2)):
        out = jnp.where(lane == pos, val, out)
    route_ref[...] = out
    total = carry_ref[0:1, :] + jnp.sum(both, axis=0, keepdims=True)
    carry_ref[...] = jnp.broadcast_to(total, carry_ref.shape)
    counts_ref[...] = carry_ref[...]


def _router(x32, w_r, b_r, tm):
    t, d = x32.shape
    tm = min(tm, t)
    ids = lax.broadcasted_iota(jnp.int32, (tm, tm), 0)
    tri = (ids > ids.T).astype(_BF16)
    return pl.pallas_call(
        functools.partial(_router_kernel, tm=tm),
        out_shape=(jax.ShapeDtypeStruct((t, _LANES), _F32), jax.ShapeDtypeStruct((8, _LANES), _F32)),
        grid=(t // tm,),
        in_specs=[pl.BlockSpec((tm, d), lambda i: (i, 0)),
                  pl.BlockSpec((d, _LANES), lambda i: (0, 0)),
                  pl.BlockSpec((1, _LANES), lambda i: (0, 0)),
                  pl.BlockSpec((tm, tm), lambda i: (0, 0))],
        out_specs=(pl.BlockSpec((tm, _LANES), lambda i: (i, 0)),
                   pl.BlockSpec((8, _LANES), lambda i: (0, 0))),
        scratch_shapes=[pltpu.VMEM((8, _LANES), _F32)],
        compiler_params=_params(("arbitrary",)),
        name="moe_router",
    )(x32, w_r, b_r, tri)


_GATHER_UNROLL = 8


def _gather_rows(src_hbm, row_of, dst_ref, sem, n):
    def issue(r, _):
        pltpu.make_async_copy(src_hbm.at[pl.ds(row_of(r), 1)], dst_ref.at[pl.ds(r, 1)], sem).start()
        return 0

    lax.fori_loop(0, n, issue, 0, unroll=_GATHER_UNROLL)


def _wait_rows(src_hbm, dst_ref, sem, n):
    pltpu.make_async_copy(src_hbm.at[pl.ds(0, n)], dst_ref, sem).wait()


def _expert_kernel(te_ref, nact_ref, tok_ref, x_hbm, wg_ref, wu_ref, wd_ref, o_ref, buf_ref, sems, *, tm):
    i = pl.program_id(0)
    nact = nact_ref[0]
    slot = lax.rem(i, 2)

    def start(tile, s):
        base = tile * tm
        _gather_rows(x_hbm, lambda r: tok_ref[base + r], buf_ref.at[s], sems.at[s], tm)

    @pl.when(i == 0)
    def _():
        start(0, 0)

    @pl.when(i + 1 < nact)
    def _():
        start(i + 1, 1 - slot)

    @pl.when(i < nact)
    def _():
        _wait_rows(x_hbm, buf_ref.at[slot], sems.at[slot], tm)
        x = buf_ref[slot].astype(_BF16)
        hg = _dot(x, wg_ref[0])
        hu = _dot(x, wu_ref[0])
        h = (hg * _sigmoid(hg)) * hu
        o_ref[...] = _dot(h.astype(_BF16), wd_ref[0])

    @pl.when(i >= nact)
    def _():
        o_ref[...] = jnp.zeros_like(o_ref)


def _expert_ffn(x32, tok_of_slot, w_gate, w_up, w_down, layer, tile_expert, nact, n_tiles, tm):
    t, d = x32.shape
    f = w_gate.shape[3]
    return pl.pallas_call(
        functools.partial(_expert_kernel, tm=tm),
        out_shape=jax.ShapeDtypeStruct((n_tiles * tm, d), _F32),
        grid_spec=pltpu.PrefetchScalarGridSpec(
            num_scalar_prefetch=3,
            grid=(n_tiles,),
            in_specs=[pl.BlockSpec(memory_space=pl.ANY),
                      pl.BlockSpec((None, 1, d, f), lambda i, te, na, tok: (layer, te[i], 0, 0)),
                      pl.BlockSpec((None, 1, d, f), lambda i, te, na, tok: (layer, te[i], 0, 0)),
                      pl.BlockSpec((None, 1, f, d), lambda i, te, na, tok: (layer, te[i], 0, 0))],
            out_specs=pl.BlockSpec((tm, d), lambda i, te, na, tok: (i, 0)),
            scratch_shapes=[pltpu.VMEM((2, tm, d), _F32), pltpu.SemaphoreType.DMA((2,))]),
        compiler_params=_params(("arbitrary",)),
        name="moe_expert_ffn",
    )(tile_expert, nact, tok_of_slot, x32, w_gate, w_up, w_down)


def _combine_kernel(pos_ref, ys_hbm, route_ref, x_ref, g_ref, b_ref, o32_ref, o16_ref, buf_ref, sems,
                    *, tc, alpha):
    i = pl.program_id(0)
    slot = lax.rem(i, 2)

    def start(step, s):
        base = step * tc
        for k in range(2):
            _gather_rows(ys_hbm, lambda r, k=k: pos_ref[2 * (base + r) + k], buf_ref.at[s, k],
                         sems.at[s, k], tc)

    @pl.when(i == 0)
    def _():
        start(0, 0)

    @pl.when(i + 1 < pl.num_programs(0))
    def _():
        start(i + 1, 1 - slot)

    for k in range(2):
        _wait_rows(ys_hbm, buf_ref.at[slot, k], sems.at[slot, k], tc)
    route = route_ref[...]
    w1, w2 = route[:, 2:3], route[:, 3:4]
    y = alpha * x_ref[...] + (w1 * buf_ref[slot, 0] + w2 * buf_ref[slot, 1])
    out = _layer_norm_rows(y, g_ref[...], b_ref[...])
    o32_ref[...] = out
    o16_ref[...] = out.astype(_BF16)


def _combine_norm(ys, pos_flat, route, x32, g, b, alpha, tc):
    t, d = x32.shape
    tc = min(tc, t)
    row = lambda: pl.BlockSpec((tc, d), lambda i, pos: (i, 0))
    vec = lambda: pl.BlockSpec((1, d), lambda i, pos: (0, 0))
    return pl.pallas_call(
        functools.partial(_combine_kernel, tc=tc, alpha=alpha),
        out_shape=(jax.ShapeDtypeStruct((t, d), _F32), jax.ShapeDtypeStruct((t, d), _BF16)),
        grid_spec=pltpu.PrefetchScalarGridSpec(
            num_scalar_prefetch=1,
            grid=(t // tc,),
            in_specs=[pl.BlockSpec(memory_space=pl.ANY),
                      pl.BlockSpec((tc, _LANES), lambda i, pos: (i, 0)), row(), vec(), vec()],
            out_specs=(row(), row()),
            scratch_shapes=[pltpu.VMEM((2, 2, tc, d), _F32), pltpu.SemaphoreType.DMA((2, 2))]),
        compiler_params=_params(("arbitrary",)),
        name="moe_combine_norm",
    )(pos_flat, ys, route, x32, g.reshape(1, d), b.reshape(1, d))


def _moe_layer(x32, w_rg, b_rg, w_re, b_re, w_gate, w_up, w_down, layer, ln_g, ln_b, alpha, tm_r, tm_e, tc):
    t, d = x32.shape
    pad = _LANES - _N_GROUPS - _N_EXPERTS
    w_r = jnp.pad(jnp.concatenate([w_rg, w_re], axis=1).astype(_F32), ((0, 0), (0, pad)))
    b_r = jnp.pad(jnp.concatenate([b_rg, b_re]).astype(_F32), (0, pad)).reshape(1, _LANES)
    route, counts = _router(x32, w_r, b_r, tm_r)

    counts = counts[0, :_N_EXPERTS].astype(jnp.int32)
    padded = ((counts + tm_e - 1) // tm_e) * tm_e
    ends = jnp.cumsum(padded)
    starts = ends - padded
    eid = route[:, 0:2].astype(jnp.int32)
    pos = starts[eid] + route[:, 4:6].astype(jnp.int32)
    n_tiles = (2 * t) // tm_e + _N_EXPERTS
    nact = (ends[-1] // tm_e).astype(jnp.int32).reshape(1)
    tile_start = jnp.minimum(jnp.arange(n_tiles, dtype=jnp.int32), nact[0] - 1) * tm_e
    tile_expert = jnp.minimum(jnp.sum((ends[None, :] <= tile_start[:, None]).astype(jnp.int32), axis=1),
                              _N_EXPERTS - 1)
    tok = jnp.broadcast_to(jnp.arange(t, dtype=jnp.int32)[:, None], (t, 2))
    tok_of_slot = jnp.zeros((n_tiles * tm_e,), jnp.int32).at[pos.reshape(-1)].set(tok.reshape(-1))

    ys = _expert_ffn(x32, tok_of_slot, w_gate, w_up, w_down, layer, tile_expert, nact, n_tiles, tm_e)
    return _combine_norm(ys, pos.reshape(-1), route, x32, ln_g, ln_b, alpha, tc)


def kernel(x, w_in, conv_w, gdn_a_log, gdn_dt_bias, gdn_norm_w, diff_lambda_q1, diff_lambda_k1,
           diff_lambda_q2, diff_lambda_k2, diff_subln_w, w_branch_sba, w_branch_gdn, w_branch_diff,
           w_out, ln1_g, ln1_b, w_router_group, b_router_group, w_router_expert, b_router_expert,
           w_expert_gate, w_expert_up, w_expert_down, ln2_g, ln2_b):
    bsz, seq, d = x.shape
    depth = w_in.shape[0]
    t = bsz * seq
    heads = d // 256
    w = heads * _HEAD_DIM
    alpha = (2 * depth) ** 0.25
    big = t >= 8192
    tm_e = 256 if big else 64

    tables = _rope_tables(seq)
    ab0 = 3 * w + 4 * w
    ab1 = ab0 + 2 * heads

    w_t = jnp.swapaxes(w_in, 1, 2).astype(_BF16)
    wb_sba, wb_gdn, wb_diff = (wb.astype(_BF16) for wb in (w_branch_sba, w_branch_gdn, w_branch_diff))
    w_out16 = w_out.astype(_BF16)
    w_gate16, w_up16, w_down16 = (we.astype(_BF16) for we in (w_expert_gate, w_expert_up, w_expert_down))

    x32 = x.reshape(t, d)
    x16 = x32.astype(_BF16)
    for l in range(depth):
        proj_a = _matmul_nt(x16, w_t, l, 0, ab0, _BF16, 1024, 1024, "in_proj_a")
        proj_b = _matmul_nt(x16, w_t, l, ab1, w_t.shape[1] - ab1, _BF16, 1024, 1024, "in_proj_b")
        ab = _matmul_nt(x16, w_t, l, ab0, ab1 - ab0, _F32, 1024, _LANES, "in_proj_ab")

        y_sba = _stick_breaking(proj_a, bsz, seq, heads, 512, 256)
        y_gdn = _gated_deltanet(proj_a, ab, conv_w[l], gdn_a_log[l], gdn_dt_bias[l], gdn_norm_w[l],
                                bsz, seq, heads, 256, 8)
        q_r, k_r = _rope(proj_b, tables, bsz, seq, heads, 512)
        lam_vecs = jnp.stack([diff_lambda_q1[l], diff_lambda_k1[l], diff_lambda_q2[l],
                              diff_lambda_k2[l]]).astype(_F32)
        lam_init = 0.8 - 0.6 * math.exp(-0.3 * l)
        y_diff = _diff_attention(q_r, k_r, proj_b, lam_vecs, diff_subln_w[l].astype(_F32), lam_init,
                                 bsz, seq, heads, 512, 256)

        merged = _branch_merge(y_sba, y_gdn, y_diff, wb_sba, wb_gdn, wb_diff, l, proj_b, 3 * w, 512, 512)
        h = _outproj_residual(merged, w_out16, l, x32, alpha, 1024, 512)
        x32, x16 = _layer_norm(h, ln1_g[l], ln1_b[l], 256)

        x32, x16 = _moe_layer(x32, w_router_group[l], b_router_group[l], w_router_expert[l],
                              b_router_expert[l], w_gate16, w_up16, w_down16, l,
                              ln2_g[l], ln2_b[l], alpha, 512, tm_e, 128)
    return x32.reshape(bsz, seq, d)
```

```python
import functools
import math

import jax
import jax.numpy as jnp
from jax import lax
from jax.experimental import pallas as pl
from jax.experimental.pallas import tpu as pltpu

_F32 = jnp.float32
_BF16 = jnp.bfloat16

_LANES = 128
_VMEM_LIMIT = 56 * 1024 * 1024
_HEAD_DIM = 128
_DIFF_DH = 64
_GDN_CHUNK = 64
_GDN_SUB = 16
_CONV_K = 4
_ROPE_THETA = 10000.0
_N_GROUPS = 4
_EXPERTS_PER_GROUP = 8
_N_EXPERTS = _N_GROUPS * _EXPERTS_PER_GROUP
_LN_EPS = 1e-5
_NORM_EPS = 1e-6
_NEG = -1e30
_LOG2E = 1.4426950408889634


def _params(sem):
    return pltpu.CompilerParams(dimension_semantics=sem, vmem_limit_bytes=_VMEM_LIMIT)


def _nt_dot(a, b):
    return lax.dot_general(a, b, (((1,), (1,)), ((), ())), preferred_element_type=_F32)


def _dot(a, b):
    return jnp.dot(a, b, preferred_element_type=_F32)


def _split2(a):
    hi = a.astype(_BF16)
    lo = (a - hi.astype(_F32)).astype(_BF16)
    return hi, lo


def _dot_x3(a, b, nt=False):
    f = _nt_dot if nt else _dot
    ah, al = _split2(a)
    bh, bl = _split2(b)
    return f(ah, bh) + (f(ah, bl) + f(al, bh))


def _dot_exact_lhs(a_bf16, b, nt=False):
    f = _nt_dot if nt else _dot
    b1 = b.astype(_BF16)
    r1 = b - b1.astype(_F32)
    b2 = r1.astype(_BF16)
    b3 = (r1 - b2.astype(_F32)).astype(_BF16)
    return f(a_bf16, b1) + (f(a_bf16, b2) + f(a_bf16, b3))


def _sigmoid(x):
    return 1.0 / (1.0 + jnp.exp(-x))


def _softplus(x):
    return jnp.maximum(x, 0.0) + jnp.log(1.0 + jnp.exp(-jnp.abs(x)))


def _mm_kernel(x_ref, w_ref, o_ref):
    o_ref[...] = _nt_dot(x_ref[...], w_ref[0]).astype(o_ref.dtype)


def _matmul_nt(x, w, layer, row0, n, out_dtype, tm, tn, name):
    m, k = x.shape
    tm, tn = min(tm, m), min(tn, n)
    while n % tn:
        tn //= 2
    return pl.pallas_call(
        _mm_kernel,
        out_shape=jax.ShapeDtypeStruct((m, n), out_dtype),
        grid=(m // tm, n // tn),
        in_specs=[pl.BlockSpec((tm, k), lambda i, j: (i, 0)),
                  pl.BlockSpec((pl.Element(1), pl.Element(tn), pl.Element(k)),
                               lambda i, j: (layer, pl.multiple_of(row0 + j * tn, math.gcd(row0, tn)), 0))],
        out_specs=pl.BlockSpec((tm, tn), lambda i, j: (i, j)),
        compiler_params=_params(("parallel", "arbitrary")),
        name=name,
    )(x, w)


def _merge_kernel(ys_ref, yg_ref, yd_ref, ws_ref, wg_ref, wd_ref, gs_ref, gg_ref, gd_ref, o_ref):
    acc = _sigmoid(gs_ref[...].astype(_F32)) * _dot(ys_ref[...], ws_ref[...])
    acc += _sigmoid(gg_ref[...].astype(_F32)) * _dot(yg_ref[...], wg_ref[...])
    acc += _sigmoid(gd_ref[...].astype(_F32)) * _dot(yd_ref[...], wd_ref[...])
    o_ref[...] = acc.astype(o_ref.dtype)


def _branch_merge(y_sba, y_gdn, y_diff, wb_sba, wb_gdn, wb_diff, layer, proj, gate_col0, tm, tn):
    t, w = y_sba.shape
    d = wb_sba.shape[2]
    tm, tn = min(tm, t), min(tn, d)
    g0 = gate_col0 // tn
    nd = d // tn
    y_spec = pl.BlockSpec((tm, w), lambda i, j: (i, 0))
    w_spec = pl.BlockSpec((None, w, tn), lambda i, j: (layer, 0, j))

    def gate_spec(b):
        return pl.BlockSpec((tm, tn), lambda i, j: (i, g0 + b * nd + j))

    return pl.pallas_call(
        _merge_kernel,
        out_shape=jax.ShapeDtypeStruct((t, d), _BF16),
        grid=(t // tm, nd),
        in_specs=[y_spec, y_spec, y_spec, w_spec, w_spec, w_spec,
                  gate_spec(0), gate_spec(1), gate_spec(2)],
        out_specs=pl.BlockSpec((tm, tn), lambda i, j: (i, j)),
        compiler_params=_params(("parallel", "arbitrary")),
        name="branch_merge",
    )(y_sba, y_gdn, y_diff, wb_sba, wb_gdn, wb_diff, proj, proj, proj)


def _outproj_kernel(m_ref, w_ref, x_ref, o_ref, *, alpha):
    o_ref[...] = alpha * x_ref[...] + _dot(m_ref[...], w_ref[...])


def _outproj_residual(merged, w_out, layer, x, alpha, tm, tn):
    t, d = merged.shape
    tm, tn = min(tm, t), min(tn, d)
    return pl.pallas_call(
        functools.partial(_outproj_kernel, alpha=alpha),
        out_shape=jax.ShapeDtypeStruct((t, d), _F32),
        grid=(t // tm, d // tn),
        in_specs=[pl.BlockSpec((tm, d), lambda i, j: (i, 0)),
                  pl.BlockSpec((None, d, tn), lambda i, j: (layer, 0, j)),
                  pl.BlockSpec((tm, tn), lambda i, j: (i, j))],
        out_specs=pl.BlockSpec((tm, tn), lambda i, j: (i, j)),
        compiler_params=_params(("parallel", "arbitrary")),
        name="outproj_residual",
    )(merged, w_out, x)


def _layer_norm_rows(y, g, b):
    mu = jnp.mean(y, axis=-1, keepdims=True)
    yc = y - mu
    var = jnp.mean(yc * yc, axis=-1, keepdims=True)
    return yc * lax.rsqrt(var + _LN_EPS) * g + b


def _ln_kernel(y_ref, g_ref, b_ref, o32_ref, o16_ref):
    out = _layer_norm_rows(y_ref[...], g_ref[...], b_ref[...])
    o32_ref[...] = out
    o16_ref[...] = out.astype(_BF16)


def _layer_norm(y, g, b, tr):
    t, d = y.shape
    tr = min(tr, t)
    row = pl.BlockSpec((tr, d), lambda i: (i, 0))
    vec = pl.BlockSpec((1, d), lambda i: (0, 0))
    return pl.pallas_call(
        _ln_kernel,
        out_shape=(jax.ShapeDtypeStruct((t, d), _F32), jax.ShapeDtypeStruct((t, d), _BF16)),
        grid=(t // tr,),
        in_specs=[row, vec, vec],
        out_specs=(row, row),
        compiler_params=_params(("parallel",)),
        name="layer_norm",
    )(y, g.reshape(1, d), b.reshape(1, d))


def _neg_abs(x):
    bits = lax.bitcast_convert_type(x, jnp.uint32) | jnp.uint32(0x80000000)
    return lax.bitcast_convert_type(bits, _F32)


_SWEEP_UNROLLS = (8, 4, 2)


def _causal_sweep(run, n_blocks, carry):
    done = 0
    for un in _SWEEP_UNROLLS:
        trips = (n_blocks - done) // un
        first = n_blocks - 1 - done
        carry = lax.fori_loop(0, trips, lambda i, c, un=un, first=first: run(first - un * i, c, un, False),
                              carry)
        done = done + trips * un
    return carry


def _sba_kernel(q_ref, k_ref, v_ref, u_ref, o_ref, acc_ref, w_ref, *, tq, tk, scale):
    qi = pl.program_id(2)
    q = (q_ref[...].astype(_F32) * (scale * _LOG2E)).astype(_BF16)
    u = u_ref[...]
    n_diag = tq // tk
    acc_ref[...] = jnp.zeros_like(acc_ref)

    def put_scores(slot, kb, r0):
        ks = pl.multiple_of(jnp.maximum(kb, 0) * tk, tk)
        w_ref[slot, r0:, :] = _nt_dot(q[r0:, :], k_ref[pl.ds(ks, tk), :])

    def consume(slot, kb, carry, masked, r0):
        v = v_ref[pl.ds(pl.multiple_of(kb * tk, tk), tk), :]
        w = w_ref[slot, r0:, :]
        sp = jnp.maximum(w, 0.0) + jnp.log(1.0 + jnp.exp2(_neg_abs(w))) * _LOG2E
        if masked:
            rows = qi * tq + r0 + lax.broadcasted_iota(jnp.int32, (tq - r0, tk), 0)
            cols = kb * tk + lax.broadcasted_iota(jnp.int32, (tq - r0, tk), 1)
            strict = cols < rows
            sp = jnp.where(strict, sp, 0.0)
        later = _dot(sp.astype(_BF16), u)
        att = jnp.exp2(((w - sp) - later) - carry[r0:, :])
        if masked:
            att = jnp.where(strict, att, 0.0)
        acc_ref[r0:, :] += _dot(att.astype(_BF16), v)
        new = carry[r0:, :] + jnp.sum(sp, axis=1, keepdims=True)
        return new if r0 == 0 else jnp.concatenate([carry[:r0, :], new], axis=0)

    def run(kb0, carry, nblk, masked):
        for j in range(nblk):
            r0 = (nblk - 1 - j) * tk if masked else 0
            put_scores((j + 1) % 2, kb0 - j - 1, max(r0 - tk, 0))
            carry = consume(j % 2, kb0 - j, carry, masked, r0)
        return carry

    top = (qi + 1) * n_diag - 1
    put_scores(0, top, (n_diag - 1) * tk)
    carry = run(top, jnp.zeros((tq, 1), _F32), n_diag, True)
    n_full = qi * n_diag
    _causal_sweep(run, n_full, carry)
    o_ref[...] = acc_ref[...].astype(o_ref.dtype)


def _stick_breaking(proj, bsz, seq, heads, tq, tk):
    t = bsz * seq
    tq, tk = min(tq, seq), min(tk, seq)
    tk = min(tk, tq)
    nq = seq // tq
    assert (tq // tk) % 2 == 0, "the two score slots alternate per key block"
    ids = lax.broadcasted_iota(jnp.int32, (tk, tk), 0)
    u = (ids > ids.T).astype(_BF16)
    return pl.pallas_call(
        functools.partial(_sba_kernel, tq=tq, tk=tk, scale=_HEAD_DIM ** -0.5),
        out_shape=jax.ShapeDtypeStruct((t, heads * _HEAD_DIM), _BF16),
        grid=(bsz, heads, nq),
        in_specs=[pl.BlockSpec((tq, _HEAD_DIM), lambda b, h, i: (b * nq + i, h)),
                  pl.BlockSpec((seq, _HEAD_DIM), lambda b, h, i: (b, heads + h)),
                  pl.BlockSpec((seq, _HEAD_DIM), lambda b, h, i: (b, 2 * heads + h)),
                  pl.BlockSpec((tk, tk), lambda b, h, i: (0, 0))],
        out_specs=pl.BlockSpec((tq, _HEAD_DIM), lambda b, h, i: (b * nq + i, h)),
        scratch_shapes=[pltpu.VMEM((tq, _HEAD_DIM), _F32), pltpu.VMEM((2, tq, tk), _F32)],
        compiler_params=_params(("parallel", "parallel", "arbitrary")),
        name="stick_breaking_attention",
    )(proj, proj, proj, u)


def _rope_kernel(q_ref, k_ref, cos_ref, sa_ref, sb_ref, qo_ref, ko_ref, *, heads, scale):
    cos, sa, sb = cos_ref[...], sa_ref[...], sb_ref[...]
    for h in range(heads):
        sl = slice(h * _HEAD_DIM, (h + 1) * _HEAD_DIM)
        for src, dst, s in ((q_ref, qo_ref, scale), (k_ref, ko_ref, 1.0)):
            x = src[:, sl].astype(_F32)
            r = (x * cos + pltpu.roll(x, _HEAD_DIM - _DIFF_DH // 2, axis=1) * sa
                 + pltpu.roll(x, _DIFF_DH // 2, axis=1) * sb)
            dst[:, sl] = (r * s).astype(dst.dtype)


def _rope_tables(seq):
    half = _DIFF_DH // 2
    pos = jnp.arange(seq, dtype=_F32)
    inv_freq = _ROPE_THETA ** (-jnp.arange(0, _DIFF_DH, 2, dtype=_F32) / _DIFF_DH)
    ang = pos[:, None] * inv_freq[None, :]
    cos, sin, zero = jnp.cos(ang), jnp.sin(ang), jnp.zeros_like(ang)
    cos_t = jnp.concatenate([cos] * 4, axis=-1)
    sa_t = jnp.concatenate([-sin, zero] * 2, axis=-1)
    sb_t = jnp.concatenate([zero, sin] * 2, axis=-1)
    return cos_t, sa_t, sb_t


def _rope(proj, tables, bsz, seq, heads, ts):
    t = bsz * seq
    w = heads * _HEAD_DIM
    ts = min(ts, seq)
    ns = seq // ts
    tab = pl.BlockSpec((ts, _HEAD_DIM), lambda i: (i % ns, 0))
    out = pl.BlockSpec((ts, w), lambda i: (i, 0))
    return pl.pallas_call(
        functools.partial(_rope_kernel, heads=heads, scale=_DIFF_DH ** -0.5 * _LOG2E),
        out_shape=(jax.ShapeDtypeStruct((t, w), _BF16), jax.ShapeDtypeStruct((t, w), _BF16)),
        grid=(t // ts,),
        in_specs=[pl.BlockSpec((ts, w), lambda i: (i, 0)),
                  pl.BlockSpec((ts, w), lambda i: (i, 1)), tab, tab, tab],
        out_specs=(out, out),
        compiler_params=_params(("parallel",)),
        name="diff_rope",
    )(proj, proj, *tables)


def _diff_kernel(lam_ref, sw_ref, q_ref, k_ref, v_ref, o_ref, acc_ref, s_ref, *, tq, tk, lam_init):
    qi = pl.program_id(2)
    q = q_ref[...]
    lane = lax.broadcasted_iota(jnp.int32, (tq, _HEAD_DIM), 1)
    zero = jnp.zeros_like(q)
    qs = (jnp.where(lane < _DIFF_DH, q, zero), jnp.where(lane >= _DIFF_DH, q, zero))
    n_diag = tq // tk
    acc_ref[...] = jnp.zeros_like(acc_ref)

    def put_scores(slot, kb, r0):
        ks = pl.multiple_of(jnp.maximum(kb, 0) * tk, tk)
        k = k_ref[pl.ds(ks, tk), :]
        for c in range(2):
            s_ref[slot, c, :, r0:] = _nt_dot(k, qs[c][r0:, :])

    def consume(slot, kb, carry, masked, r0):
        v = v_ref[pl.ds(pl.multiple_of(kb * tk, tk), tk), :]
        if masked:
            keys = kb * tk + lax.broadcasted_iota(jnp.int32, (tk, tq - r0), 0)
            qpos = qi * tq + r0 + lax.broadcasted_iota(jnp.int32, (tk, tq - r0), 1)
            causal = keys <= qpos
        out = []
        for c in range(2):
            m_prev, l_prev = carry[2 * c][:, r0:], carry[2 * c + 1][:, r0:]
            s = s_ref[slot, c, :, r0:]
            if masked:
                s = jnp.where(causal, s, _NEG)
            m_new = jnp.maximum(m_prev, jnp.max(s, axis=0, keepdims=True))
            p = jnp.exp2(s - m_new)
            alpha = jnp.exp2(m_prev - m_new)
            pv = lax.dot_general(v, p.astype(_BF16), (((0,), (0,)), ((), ())), preferred_element_type=_F32)
            acc_ref[c, :, r0:] = alpha * acc_ref[c, :, r0:] + pv
            new = [m_new, alpha * l_prev + jnp.sum(p, axis=0, keepdims=True)]
            if r0:
                new = [jnp.concatenate([carry[2 * c + i][:, :r0], new[i]], axis=1) for i in range(2)]
            out += new
        return tuple(out)

    def run(kb0, carry, nblk, masked):
        for j in range(nblk):
            r0 = (nblk - 1 - j) * tk if masked else 0
            put_scores((j + 1) % 2, kb0 - j - 1, max(r0 - tk, 0))
            carry = consume(j % 2, kb0 - j, carry, masked, r0)
        return carry

    neg = jnp.full((1, tq), _NEG, _F32)
    zero_row = jnp.zeros((1, tq), _F32)
    top = (qi + 1) * n_diag - 1
    put_scores(0, top, (n_diag - 1) * tk)
    carry = run(top, (neg, zero_row, neg, zero_row), n_diag, True)
    n_full = qi * n_diag
    carry = _causal_sweep(run, n_full, carry)

    lv = lam_ref[...]
    lam = (jnp.exp(jnp.sum(lv[0:1] * lv[1:2], axis=1, keepdims=True))
           - jnp.exp(jnp.sum(lv[2:3] * lv[3:4], axis=1, keepdims=True)) + lam_init)
    o = acc_ref[0] / carry[1] - lam * (acc_ref[1] / carry[3])
    o = o * lax.rsqrt(jnp.mean(o * o, axis=0, keepdims=True) + _LN_EPS)
    o_ref[...] = (o.T * sw_ref[...] * (1.0 - lam_init)).astype(o_ref.dtype)


def _diff_attention(q_r, k_r, proj, lam_vecs, subln_w, lam_init, bsz, seq, heads, tq, tk):
    t = bsz * seq
    tq, tk = min(tq, seq), min(tk, seq)
    tk = min(tk, tq)
    nq = seq // tq
    return pl.pallas_call(
        functools.partial(_diff_kernel, tq=tq, tk=tk, lam_init=lam_init),
        out_shape=jax.ShapeDtypeStruct((t, heads * _HEAD_DIM), _BF16),
        grid=(bsz, heads, nq),
        in_specs=[pl.BlockSpec((4, _DIFF_DH), lambda b, h, i: (0, 0)),
                  pl.BlockSpec((1, _HEAD_DIM), lambda b, h, i: (0, 0)),
                  pl.BlockSpec((tq, _HEAD_DIM), lambda b, h, i: (b * nq + i, h)),
                  pl.BlockSpec((seq, _HEAD_DIM), lambda b, h, i: (b, h)),
                  pl.BlockSpec((seq, _HEAD_DIM), lambda b, h, i: (b, 2 * heads + h))],
        out_specs=pl.BlockSpec((tq, _HEAD_DIM), lambda b, h, i: (b * nq + i, h)),
        scratch_shapes=[pltpu.VMEM((2, _HEAD_DIM, tq), _F32), pltpu.VMEM((2, 2, tk, tq), _F32)],
        compiler_params=_params(("parallel", "parallel", "arbitrary")),
        name="differential_attention",
    )(lam_vecs, subln_w.reshape(1, _HEAD_DIM), q_r, k_r, proj)


def _bdot(a, b):
    return lax.dot_general(a, b, (((2,), (1,)), ((0,), (0,))), preferred_element_type=_F32)


def _bdot_nt(a, b):
    return lax.dot_general(a, b, (((2,), (2,)), ((0,), (0,))), preferred_element_type=_F32)


def _bdot_tn(a, b):
    return lax.dot_general(a, b, (((1,), (1,)), ((0,), (0,))), preferred_element_type=_F32)


def _bdot16(a, b):
    return _bdot(a.astype(_BF16), b.astype(_BF16))


def _unit_lower_inverse(lm, eye, blockdiag):
    c = lm.shape[-1]
    md = jnp.where(blockdiag, -lm, 0.0)
    x = eye + md
    p = _bdot16(md, md)
    steps = int(math.log2(_GDN_SUB)) - 1
    for it in range(steps):
        x = x + _bdot16(x, p)
        if it + 1 < steps:
            p = _bdot16(p, p)
    n = _bdot16(x, jnp.where(blockdiag, 0.0, lm))
    y = eye - n
    pw = n
    for _ in range(int(math.log2(c // _GDN_SUB)) - 1):
        pw = _bdot16(pw, pw)
        y = y + _bdot16(y, pw)
    return _bdot16(y, x)


def _gdn_kernel(ab_ref, q_ref, k_ref, v_ref, z_ref, wq_ref, wk_ref, wv_ref, alog_ref, dtb_ref, nw_ref,
                tri_ref, o_ref, xp_ref, qs_ref, ks_ref, vs_ref, dm_ref, gc_ref, bt_ref,
                u_ref, w_ref, qd_ref, kd_ref, in_ref, dl_ref, state_ref, *, tb, gh):
    i = pl.program_id(2)
    c = _GDN_CHUNK
    nc = tb // c
    halo = 8

    @pl.when(i == 0)
    def _():
        xp_ref[:, 0:halo, :] = jnp.zeros((3, halo, gh * _HEAD_DIM), _F32)
        state_ref[...] = jnp.zeros_like(state_ref)

    @pl.when(i > 0)
    def _():
        xp_ref[:, 0:halo, :] = xp_ref[:, tb:tb + halo, :]

    for idx, (src, cw_ref, dst) in enumerate(((q_ref, wq_ref, qs_ref), (k_ref, wk_ref, ks_ref),
                                              (v_ref, wv_ref, vs_ref))):
        xp_ref[idx, halo:halo + tb, :] = src[...].astype(_F32)
        w = cw_ref[...]
        y = xp_ref[idx, halo - 3:halo - 3 + tb, :] * w[0:1, :]
        for j in range(1, _CONV_K):
            y = y + xp_ref[idx, halo - 3 + j:halo - 3 + j + tb, :] * w[j:j + 1, :]
        y = y * _sigmoid(y)
        for g in range(gh):
            yh = y[:, g * _HEAD_DIM:(g + 1) * _HEAD_DIM]
            if idx < 2:
                yh = yh * lax.rsqrt(jnp.sum(yh * yh, axis=1, keepdims=True) + _NORM_EPS)
            if idx == 0:
                yh = yh * (_HEAD_DIM ** -0.5)
            for ci in range(nc):
                dst[ci * gh + g] = yh[ci * c:(ci + 1) * c, :]

    ab = ab_ref[0]
    g_all = -jnp.exp(alog_ref[0]) * _softplus(ab + dtb_ref[0])
    beta_all = _sigmoid(ab)
    tri = tri_ref[...]
    pick = (lax.broadcasted_iota(jnp.int32, (8, _LANES), 0)
            == lax.broadcasted_iota(jnp.int32, (8, _LANES), 1)).astype(_BF16)
    for ci in range(nc):
        rows = slice(ci * c, (ci + 1) * c)
        gcum = _dot_exact_lhs(tri, g_all[rows, :])
        gcum_t = _dot_exact_lhs(pick, gcum, nt=True)
        for g in range(gh):
            gcol = gcum[:, g:g + 1]
            dm_ref[ci * gh + g] = gcol - gcum_t[g:g + 1, :]
            gc_ref[ci * gh + g] = jnp.broadcast_to(gcol, (c, _LANES))
            bt_ref[ci * gh + g] = jnp.broadcast_to(beta_all[rows, gh + g:gh + g + 1], (c, _LANES))

    ri = lax.broadcasted_iota(jnp.int32, (c, c), 0)
    cj = lax.broadcasted_iota(jnp.int32, (c, c), 1)
    incl, strict = ri >= cj, ri > cj
    sub_shift = int(math.log2(_GDN_SUB))
    blockdiag = lax.shift_right_logical(ri, sub_shift) == lax.shift_right_logical(cj, sub_shift)
    eye = (ri == cj).astype(_F32)

    gc = gc_ref[...]
    beta = bt_ref[...]
    decay = jnp.where(incl, jnp.exp(jnp.where(incl, dm_ref[...], 0.0)), 0.0)
    g_last = gc[:, c - 1:c, :]
    eg = jnp.exp(gc)
    dl_ref[...] = jnp.exp(g_last)
    q, k, v = qs_ref[...], ks_ref[...], vs_ref[...]
    k16 = k.astype(_BF16)
    kb = k * beta
    lm = jnp.where(strict, _bdot_nt(kb.astype(_BF16), k16) * decay, 0.0)
    tinv = _unit_lower_inverse(lm, eye, blockdiag)
    u_ref[...] = _bdot16(tinv, v * beta)
    w_ref[...] = _bdot16(tinv, kb * eg).astype(_BF16)
    in_ref[...] = (_bdot_nt(q.astype(_BF16), k16) * decay).astype(_BF16)
    qd_ref[...] = (q * eg).astype(_BF16)
    kd_ref[...] = (k * jnp.exp(g_last - gc)).astype(_BF16)

    nw = nw_ref[...]

    def chunk(cidx, _):
        sl = pl.ds(cidx * gh, gh)
        st = state_ref[...]
        st16 = st.astype(_BF16)
        v_new = u_ref[sl] - _bdot(w_ref[sl], st16)
        vn16 = v_new.astype(_BF16)
        o = _bdot(qd_ref[sl], st16) + _bdot(in_ref[sl], vn16)
        state_ref[...] = st * dl_ref[sl] + _bdot_tn(kd_ref[sl], vn16)
        o = o * lax.rsqrt(jnp.mean(o * o, axis=2, keepdims=True) + _NORM_EPS) * nw
        rows = pl.ds(pl.multiple_of(cidx * c, c), c)
        for g in range(gh):
            lanes = slice(g * _HEAD_DIM, (g + 1) * _HEAD_DIM)
            zf = z_ref[rows, lanes].astype(_F32)
            o_ref[rows, lanes] = (o[g] * (zf * _sigmoid(zf))).astype(o_ref.dtype)
        return 0

    lax.fori_loop(0, nc, chunk, 0)


def _gated_deltanet(proj, ab, conv_w, a_log, dt_bias, norm_w, bsz, seq, heads, tb, gh):
    t = bsz * seq
    w = heads * _HEAD_DIM
    tb = min(tb, seq)
    gh = min(gh, heads)
    assert gh <= 8, "one sublane tile holds the transposed per-head cumulative gates"
    nb = seq // tb
    ng = heads // gh
    c = _GDN_CHUNK
    nbatch = (tb // c) * gh
    ids = lax.broadcasted_iota(jnp.int32, (c, c), 0)
    tri = (ids >= ids.T).astype(_BF16)

    def group(x):
        return jnp.moveaxis(x.reshape(x.shape[0], ng, gh), 1, 0)

    lane_pad = ((0, 0), (0, 0), (0, _LANES - 2 * gh))
    ab_g = jnp.pad(jnp.concatenate([group(ab[:, :heads]), group(ab[:, heads:2 * heads])], axis=2), lane_pad)
    zeros = jnp.zeros((1, heads), _F32)
    alog_g = jnp.pad(jnp.concatenate([group(a_log.astype(_F32)[None]), group(zeros)], axis=2), lane_pad)
    dtb_g = jnp.pad(jnp.concatenate([group(dt_bias.astype(_F32)[None]), group(zeros)], axis=2), lane_pad)
    blk = (tb, gh * _HEAD_DIM)

    def col(base):
        return pl.BlockSpec(blk, lambda b, g, i: (b * nb + i, base * ng + g))

    def wcol(base):
        return pl.BlockSpec((_CONV_K, gh * _HEAD_DIM), lambda b, g, i: (0, base * ng + g))

    vec = pl.BlockSpec((1, 1, _LANES), lambda b, g, i: (g, 0, 0))
    batch_f32 = pltpu.VMEM((nbatch, c, _HEAD_DIM), _F32)
    batch_b16 = pltpu.VMEM((nbatch, c, _HEAD_DIM), _BF16)
    return pl.pallas_call(
        functools.partial(_gdn_kernel, tb=tb, gh=gh),
        out_shape=jax.ShapeDtypeStruct((t, w), _BF16),
        grid=(bsz, ng, nb),
        in_specs=[pl.BlockSpec((1, tb, _LANES), lambda b, g, i: (g, b * nb + i, 0)),
                  col(3), col(4), col(5), col(6), wcol(0), wcol(1), wcol(2), vec, vec,
                  pl.BlockSpec((1, _HEAD_DIM), lambda b, g, i: (0, 0)),
                  pl.BlockSpec((c, c), lambda b, g, i: (0, 0))],
        out_specs=pl.BlockSpec(blk, lambda b, g, i: (b * nb + i, g)),
        scratch_shapes=[pltpu.VMEM((3, tb + 8, gh * _HEAD_DIM), _F32),
                        batch_f32, batch_f32, batch_f32,
                        pltpu.VMEM((nbatch, c, c), _F32), batch_f32, batch_f32,
                        batch_f32, batch_b16, batch_b16, batch_b16,
                        pltpu.VMEM((nbatch, c, c), _BF16),
                        pltpu.VMEM((nbatch, 1, _LANES), _F32),
                        pltpu.VMEM((gh, _HEAD_DIM, _HEAD_DIM), _F32)],
        compiler_params=_params(("parallel", "parallel", "arbitrary")),
        name="gated_deltanet",
    )(ab_g, proj, proj, proj, proj, conv_w, conv_w, conv_w, alog_g, dtb_g,
      norm_w.reshape(1, _HEAD_DIM).astype(_F32), tri)


def _router_kernel(x_ref, w_ref, b_ref, tri_ref, route_ref, counts_ref, carry_ref, *, tm):
    i = pl.program_id(0)

    @pl.when(i == 0)
    def _():
        carry_ref[...] = jnp.zeros_like(carry_ref)

    logits = _dot_x3(x_ref[...], w_ref[...]) + b_ref[...]
    lane = lax.broadcasted_iota(jnp.int32, (tm, _LANES), 1)
    big = jnp.int32(_LANES)

    def top(vals):
        m = jnp.max(vals, axis=1, keepdims=True)
        idx = jnp.min(jnp.where(vals == m, lane, big), axis=1, keepdims=True)
        return m, idx

    gl = jnp.where(lane < _N_GROUPS, logits, _NEG)
    gm, gidx = top(gl)
    g_w = 1.0 / jnp.sum(jnp.exp(gl - gm), axis=1, keepdims=True)
    lane_group = lax.shift_right_arithmetic(lane - _N_GROUPS, int(math.log2(_EXPERTS_PER_GROUP)))
    el = jnp.where(lane_group == gidx, logits, _NEG)
    m1, i1 = top(el)
    m2, i2 = top(jnp.where(lane == i1, _NEG, el))
    r = jnp.exp(m2 - m1)
    w1 = g_w / (1.0 + r)
    w2 = g_w * r / (1.0 + r)
    e1, e2 = i1 - _N_GROUPS, i2 - _N_GROUPS
    oh1 = (lane == e1).astype(_F32)
    oh2 = (lane == e2).astype(_F32)
    both = oh1 + oh2
    before = _dot(tri_ref[...], both.astype(_BF16)) + carry_ref[0:1, :]
    rank1 = jnp.sum(oh1 * before, axis=1, keepdims=True)
    rank2 = jnp.sum(oh2 * before, axis=1, keepdims=True)
    out = jnp.zeros((tm, _LANES), _F32)
    for pos, val in enumerate((e1.astype(_F32), e2.astype(_F32), w1, w2, rank1, rank2)):
        out = jnp.where(lane == pos, val, out)
    route_ref[...] = out
    total = carry_ref[0:1, :] + jnp.sum(both, axis=0, keepdims=True)
    carry_ref[...] = jnp.broadcast_to(total, carry_ref.shape)
    counts_ref[...] = carry_ref[...]


def _router(x32, w_r, b_r, tm):
    t, d = x32.shape
    tm = min(tm, t)
    ids = lax.broadcasted_iota(jnp.int32, (tm, tm), 0)
    tri = (ids > ids.T).astype(_BF16)
    return pl.pallas_call(
        functools.partial(_router_kernel, tm=tm),
        out_shape=(jax.ShapeDtypeStruct((t, _LANES), _F32), jax.ShapeDtypeStruct((8, _LANES), _F32)),
        grid=(t // tm,),
        in_specs=[pl.BlockSpec((tm, d), lambda i: (i, 0)),
                  pl.BlockSpec((d, _LANES), lambda i: (0, 0)),
                  pl.BlockSpec((1, _LANES), lambda i: (0, 0)),
                  pl.BlockSpec((tm, tm), lambda i: (0, 0))],
        out_specs=(pl.BlockSpec((tm, _LANES), lambda i: (i, 0)),
                   pl.BlockSpec((8, _LANES), lambda i: (0, 0))),
        scratch_shapes=[pltpu.VMEM((8, _LANES), _F32)],
        compiler_params=_params(("arbitrary",)),
        name="moe_router",
    )(x32, w_r, b_r, tri)


_GATHER_UNROLL = 8


def _gather_rows(src_hbm, row_of, dst_ref, sem, n):
    def issue(r, _):
        pltpu.make_async_copy(src_hbm.at[pl.ds(row_of(r), 1)], dst_ref.at[pl.ds(r, 1)], sem).start()
        return 0

    lax.fori_loop(0, n, issue, 0, unroll=_GATHER_UNROLL)


def _wait_rows(src_hbm, dst_ref, sem, n):
    pltpu.make_async_copy(src_hbm.at[pl.ds(0, n)], dst_ref, sem).wait()


def _expert_kernel(te_ref, nact_ref, tok_ref, x_hbm, wg_ref, wu_ref, wd_ref, o_ref, buf_ref, sems, *, tm):
    i = pl.program_id(0)
    nact = nact_ref[0]
    slot = lax.rem(i, 2)

    def start(tile, s):
        base = tile * tm
        _gather_rows(x_hbm, lambda r: tok_ref[base + r], buf_ref.at[s], sems.at[s], tm)

    @pl.when(i == 0)
    def _():
        start(0, 0)

    @pl.when(i + 1 < nact)
    def _():
        start(i + 1, 1 - slot)

    @pl.when(i < nact)
    def _():
        _wait_rows(x_hbm, buf_ref.at[slot], sems.at[slot], tm)
        x = buf_ref[slot].astype(_BF16)
        hg = _dot(x, wg_ref[0])
        hu = _dot(x, wu_ref[0])
        h = (hg * _sigmoid(hg)) * hu
        o_ref[...] = _dot(h.astype(_BF16), wd_ref[0])

    @pl.when(i >= nact)
    def _():
        o_ref[...] = jnp.zeros_like(o_ref)


def _expert_ffn(x32, tok_of_slot, w_gate, w_up, w_down, layer, tile_expert, nact, n_tiles, tm):
    t, d = x32.shape
    f = w_gate.shape[3]
    return pl.pallas_call(
        functools.partial(_expert_kernel, tm=tm),
        out_shape=jax.ShapeDtypeStruct((n_tiles * tm, d), _F32),
        grid_spec=pltpu.PrefetchScalarGridSpec(
            num_scalar_prefetch=3,
            grid=(n_tiles,),
            in_specs=[pl.BlockSpec(memory_space=pl.ANY),
                      pl.BlockSpec((None, 1, d, f), lambda i, te, na, tok: (layer, te[i], 0, 0)),
                      pl.BlockSpec((None, 1, d, f), lambda i, te, na, tok: (layer, te[i], 0, 0)),
                      pl.BlockSpec((None, 1, f, d), lambda i, te, na, tok: (layer, te[i], 0, 0))],
            out_specs=pl.BlockSpec((tm, d), lambda i, te, na, tok: (i, 0)),
            scratch_shapes=[pltpu.VMEM((2, tm, d), _F32), pltpu.SemaphoreType.DMA((2,))]),
        compiler_params=_params(("arbitrary",)),
        name="moe_expert_ffn",
    )(tile_expert, nact, tok_of_slot, x32, w_gate, w_up, w_down)


def _combine_kernel(pos_ref, ys_hbm, route_ref, x_ref, g_ref, b_ref, o32_ref, o16_ref, buf_ref, sems,
                    *, tc, alpha):
    i = pl.program_id(0)
    slot = lax.rem(i, 2)

    def start(step, s):
        base = step * tc
        for k in range(2):
            _gather_rows(ys_hbm, lambda r, k=k: pos_ref[2 * (base + r) + k], buf_ref.at[s, k],
                         sems.at[s, k], tc)

    @pl.when(i == 0)
    def _():
        start(0, 0)

    @pl.when(i + 1 < pl.num_programs(0))
    def _():
        start(i + 1, 1 - slot)

    for k in range(2):
        _wait_rows(ys_hbm, buf_ref.at[slot, k], sems.at[slot, k], tc)
    route = route_ref[...]
    w1, w2 = route[:, 2:3], route[:, 3:4]
    y = alpha * x_ref[...] + (w1 * buf_ref[slot, 0] + w2 * buf_ref[slot, 1])
    out = _layer_norm_rows(y, g_ref[...], b_ref[...])
    o32_ref[...] = out
    o16_ref[...] = out.astype(_BF16)


def _combine_norm(ys, pos_flat, route, x32, g, b, alpha, tc):
    t, d = x32.shape
    tc = min(tc, t)
    row = lambda: pl.BlockSpec((tc, d), lambda i, pos: (i, 0))
    vec = lambda: pl.BlockSpec((1, d), lambda i, pos: (0, 0))
    return pl.pallas_call(
        functools.partial(_combine_kernel, tc=tc, alpha=alpha),
        out_shape=(jax.ShapeDtypeStruct((t, d), _F32), jax.ShapeDtypeStruct((t, d), _BF16)),
        grid_spec=pltpu.PrefetchScalarGridSpec(
            num_scalar_prefetch=1,
            grid=(t // tc,),
            in_specs=[pl.BlockSpec(memory_space=pl.ANY),
                      pl.BlockSpec((tc, _LANES), lambda i, pos: (i, 0)), row(), vec(), vec()],
            out_specs=(row(), row()),
            scratch_shapes=[pltpu.VMEM((2, 2, tc, d), _F32), pltpu.SemaphoreType.DMA((2, 2))]),
        compiler_params=_params(("arbitrary",)),
        name="moe_combine_norm",
    )(pos_flat, ys, route, x32, g.reshape(1, d), b.reshape(1, d))


def _moe_layer(x32, w_rg, b_rg, w_re, b_re, w_gate, w_up, w_down, layer, ln_g, ln_b, alpha, tm_r, tm_e, tc):
    t, d = x32.shape
    pad = _LANES - _N_GROUPS - _N_EXPERTS
    w_r = jnp.pad(jnp.concatenate([w_rg, w_re], axis=1).astype(_F32), ((0, 0), (0, pad)))
    b_r = jnp.pad(jnp.concatenate([b_rg, b_re]).astype(_F32), (0, pad)).reshape(1, _LANES)
    route, counts = _router(x32, w_r, b_r, tm_r)

    counts = counts[0, :_N_EXPERTS].astype(jnp.int32)
    padded = ((counts + tm_e - 1) // tm_e) * tm_e
    ends = jnp.cumsum(padded)
    starts = ends - padded
    eid = route[:, 0:2].astype(jnp.int32)
    pos = starts[eid] + route[:, 4:6].astype(jnp.int32)
    n_tiles = (2 * t) // tm_e + _N_EXPERTS
    nact = (ends[-1] // tm_e).astype(jnp.int32).reshape(1)
    tile_start = jnp.minimum(jnp.arange(n_tiles, dtype=jnp.int32), nact[0] - 1) * tm_e
    tile_expert = jnp.minimum(jnp.sum((ends[None, :] <= tile_start[:, None]).astype(jnp.int32), axis=1),
                              _N_EXPERTS - 1)
    tok = jnp.broadcast_to(jnp.arange(t, dtype=jnp.int32)[:, None], (t, 2))
    tok_of_slot = jnp.zeros((n_tiles * tm_e,), jnp.int32).at[pos.reshape(-1)].set(tok.reshape(-1))

    ys = _expert_ffn(x32, tok_of_slot, w_gate, w_up, w_down, layer, tile_expert, nact, n_tiles, tm_e)
    return _combine_norm(ys, pos.reshape(-1), route, x32, ln_g, ln_b, alpha, tc)


def kernel(x, w_in, conv_w, gdn_a_log, gdn_dt_bias, gdn_norm_w, diff_lambda_q1, diff_lambda_k1,
           diff_lambda_q2, diff_lambda_k2, diff_subln_w, w_branch_sba, w_branch_gdn, w_branch_diff,
           w_out, ln1_g, ln1_b, w_router_group, b_router_group, w_router_expert, b_router_expert,
           w_expert_gate, w_expert_up, w_expert_down, ln2_g, ln2_b):
    bsz, seq, d = x.shape
    depth = w_in.shape[0]
    t = bsz * seq
    heads = d // 256
    w = heads * _HEAD_DIM
    alpha = (2 * depth) ** 0.25
    big = t >= 8192
    tm_e = 256 if big else 64

    tables = _rope_tables(seq)
    ab0 = 3 * w + 4 * w
    ab1 = ab0 + 2 * heads

    w_t = jnp.swapaxes(w_in, 1, 2).astype(_BF16)
    wb_sba, wb_gdn, wb_diff = (wb.astype(_BF16) for wb in (w_branch_sba, w_branch_gdn, w_branch_diff))
    w_out16 = w_out.astype(_BF16)
    w_gate16, w_up16, w_down16 = (we.astype(_BF16) for we in (w_expert_gate, w_expert_up, w_expert_down))

    x32 = x.reshape(t, d)
    x16 = x32.astype(_BF16)
    for l in range(depth):
        proj_a = _matmul_nt(x16, w_t, l, 0, ab0, _BF16, 1024, 1024, "in_proj_a")
        proj_b = _matmul_nt(x16, w_t, l, ab1, w_t.shape[1] - ab1, _BF16, 1024, 1024, "in_proj_b")
        ab = _matmul_nt(x16, w_t, l, ab0, ab1 - ab0, _F32, 1024, _LANES, "in_proj_ab")

        y_sba = _stick_breaking(proj_a, bsz, seq, heads, 512, 256)
        y_gdn = _gated_deltanet(proj_a, ab, conv_w[l], gdn_a_log[l], gdn_dt_bias[l], gdn_norm_w[l],
                                bsz, seq, heads, 256, 8)
        q_r, k_r = _rope(proj_b, tables, bsz, seq, heads, 512)
        lam_vecs = jnp.stack([diff_lambda_q1[l], diff_lambda_k1[l], diff_lambda_q2[l],
                              diff_lambda_k2[l]]).astype(_F32)
        lam_init = 0.8 - 0.6 * math.exp(-0.3 * l)
        y_diff = _diff_attention(q_r, k_r, proj_b, lam_vecs, diff_subln_w[l].astype(_F32), lam_init,
                                 bsz, seq, heads, 512, 256)

        merged = _branch_merge(y_sba, y_gdn, y_diff, wb_sba, wb_gdn, wb_diff, l, proj_b, 3 * w, 1024, 512)
        h = _outproj_residual(merged, w_out16, l, x32, alpha, 1024, 512)
        x32, x16 = _layer_norm(h, ln1_g[l], ln1_b[l], 256)

        x32, x16 = _moe_layer(x32, w_router_group[l], b_router_group[l], w_router_expert[l],
                              b_router_expert[l], w_gate16, w_up16, w_down16, l,
                              ln2_g[l], ln2_b[l], alpha, 512, tm_e, 256)
    return x32.reshape(bsz, seq, d)
```
